```python
import jax, jax.numpy as jnp
from jax import lax
import numpy as np

D_MODEL = 2048
BATCH = 4
SEQ = 8192
DEPTH = 1
DEC_BATCH = 1
DEC_SEQ = 8192
PAST_LEN = 128

D_A = D_MODEL // 2
POOL_WINDOWS = (2, 4, 8, 16)
N_POOL_GROUPS = len(POOL_WINDOWS)
A_GROUP = D_A // N_POOL_GROUPS
D_B = D_MODEL // 2
CHUNK = 128
SGU_HEADS = 8
SGU_HEAD_DIM = D_B // SGU_HEADS
D_IN = D_A + 2 * D_B + 2 * D_MODEL
PEER_HEADS = 8
PEER_DQ = 256
PEER_DHALF = PEER_DQ // 2
N_KEYS = 128
N_EXPERTS = N_KEYS * N_KEYS
PEER_TOPK = 16
PEER_TOKEN_BLOCK = 128
ALPHA = (2.0 * DEPTH) ** 0.25
BETA = (8.0 * DEPTH) ** -0.25
LN_EPS = 1e-5

kernel_name = "hybrid_pool_sgu_peer_encoder"


def norm0(x):
    xf = x.astype(jnp.float32)
    mu = jnp.mean(xf, axis=-1, keepdims=True)
    var = jnp.mean(jnp.square(xf - mu), axis=-1, keepdims=True)
    return ((xf - mu) * lax.rsqrt(var + LN_EPS)).astype(x.dtype)


def layer_norm(x, g, b):
    return norm0(x) * g + b


def multiscale_pool(a, pool_w, pool_scale):
    B, S, _ = a.shape
    ag = a.reshape(B, S, N_POOL_GROUPS, A_GROUP).astype(jnp.float32)
    cs = jnp.concatenate([jnp.zeros((B, 1, N_POOL_GROUPS, A_GROUP), jnp.float32),
                          jnp.cumsum(ag, axis=1)], axis=1)
    t = jnp.arange(S, dtype=jnp.int32)[:, None]
    half = jnp.array([w // 2 for w in POOL_WINDOWS], dtype=jnp.int32)[None, :]
    lo = jnp.clip(t - half, 0, S)
    hi = jnp.clip(t + half, 0, S)
    gidx = jnp.arange(N_POOL_GROUPS, dtype=jnp.int32)[None, :]
    win_sum = cs[:, hi, gidx, :] - cs[:, lo, gidx, :]
    cnt = (hi - lo).astype(jnp.float32)[None, :, :, None]
    pooled = (win_sum / cnt - ag).astype(a.dtype)
    mixed = jnp.einsum('bsgc,gcd->bsgd', pooled, pool_w)
    return mixed.reshape(B, S, D_A) * pool_scale


def spatial_gating(u, v, ln_g, ln_b, w_s, b_s):
    B, S, _ = v.shape
    v = layer_norm(v, ln_g, ln_b)
    vh = v.reshape(B, S // CHUNK, CHUNK, SGU_HEADS, SGU_HEAD_DIM)
    mixed = jnp.einsum('hpq,bnqhc->bnphc', w_s, vh) + b_s.T[None, None, :, :, None]
    return u * mixed.reshape(B, S, D_B)


def peer(h, w_q, keys, u_tab, v_tab):
    B, S, D = h.shape
    T = B * S
    hf = h.reshape(T, D)
    q = (hf @ w_q).reshape(T, PEER_HEADS, 2, PEER_DHALF).astype(jnp.float32)
    s = jnp.einsum('thkc,hknc->thkn', q, keys.astype(jnp.float32))
    s1, i1 = lax.top_k(s[:, :, 0], PEER_TOPK)
    s2, i2 = lax.top_k(s[:, :, 1], PEER_TOPK)
    cand_s = (s1[..., :, None] + s2[..., None, :]).reshape(T, PEER_HEADS, PEER_TOPK * PEER_TOPK)
    cand_i = (i1[..., :, None] * N_KEYS + i2[..., None, :]).reshape(T, PEER_HEADS, PEER_TOPK * PEER_TOPK)
    top_s, pos = lax.top_k(cand_s, PEER_TOPK)
    idx = jnp.take_along_axis(cand_i, pos, axis=-1)
    g = jax.nn.softmax(top_s, axis=-1).astype(h.dtype)
    K = PEER_HEADS * PEER_TOPK
    nblk = T // PEER_TOKEN_BLOCK
    h_blk = hf.reshape(nblk, PEER_TOKEN_BLOCK, D)
    i_blk = idx.reshape(nblk, PEER_TOKEN_BLOCK, K)
    g_blk = g.reshape(nblk, PEER_TOKEN_BLOCK, K)

    def expert_block(args):
        hc, ic, gc = args
        act = jax.nn.gelu(jnp.einsum('ckd,cd->ck', u_tab[ic], hc))
        return jnp.einsum('ck,ckd->cd', gc * act, v_tab[ic])

    out = lax.map(expert_block, (h_blk, i_blk, g_blk))
    return out.reshape(B, S, D)


def encoder_layer(x, c, w_ada, b_ada, w_in, pool_w, pool_scale, sgu_ln_g, sgu_ln_b, sgu_w, sgu_b,
                  w_up_a, w_up_b, w_out, ln1_g, ln1_b, peer_wq, peer_keys, peer_u, peer_v,
                  ln2_g, ln2_b):
    B = x.shape[0]
    mod = (jax.nn.silu(c) @ w_ada + b_ada).reshape(B, 6, 1, D_MODEL)
    shift1, scale1, gate1, shift2, scale2, gate2 = [mod[:, i] for i in range(6)]
    h = norm0(x) * (1.0 + scale1) + shift1
    p = h @ w_in
    a = p[..., :D_A]
    u = p[..., D_A:D_A + D_B]
    v = p[..., D_A + D_B:D_A + 2 * D_B]
    ga = p[..., D_A + 2 * D_B:D_A + 2 * D_B + D_MODEL]
    gb = p[..., D_A + 2 * D_B + D_MODEL:]
    out_a = multiscale_pool(a, pool_w, pool_scale)
    out_b = spatial_gating(jax.nn.gelu(u), jax.nn.gelu(v), sgu_ln_g, sgu_ln_b, sgu_w, sgu_b)
    merged = jax.nn.sigmoid(ga) * (out_a @ w_up_a) + jax.nn.sigmoid(gb) * (out_b @ w_up_b)
    y = merged @ w_out
    x = layer_norm(ALPHA * x + gate1 * y, ln1_g, ln1_b)
    h2 = norm0(x) * (1.0 + scale2) + shift2
    y2 = peer(h2, peer_wq, peer_keys, peer_u, peer_v)
    return layer_norm(ALPHA * x + gate2 * y2, ln2_g, ln2_b)


def trunk(x, c, w_ada, b_ada, w_in, pool_w, pool_scale, sgu_ln_g, sgu_ln_b, sgu_w, sgu_b,
          w_up_a, w_up_b, w_out, ln1_g, ln1_b, peer_wq, peer_keys, peer_u, peer_v, ln2_g, ln2_b):
    for l in range(DEPTH):
        x = encoder_layer(x, c, w_ada[l], b_ada[l], w_in[l], pool_w[l], pool_scale[l],
                          sgu_ln_g[l], sgu_ln_b[l], sgu_w[l], sgu_b[l], w_up_a[l], w_up_b[l],
                          w_out[l], ln1_g[l], ln1_b[l], peer_wq[l], peer_keys[l], peer_u[l],
                          peer_v[l], ln2_g[l], ln2_b[l])
    return x


def setup_inputs(seed: int = 0) -> dict:
    key = jax.random.key(seed)
    ks = jax.random.split(key, 26)
    f32 = jnp.float32
    L, D = DEPTH, D_MODEL

    def nrm(k, shape, scale):
        return jax.random.normal(k, shape, f32) * scale

    return {
        "x_prompt": nrm(ks[0], (BATCH, SEQ, D), 1.0),
        "x_sample": nrm(ks[1], (DEC_BATCH, DEC_SEQ, D), 1.0),
        "c_prompt": nrm(ks[2], (BATCH, D), 1.0),
        "c_sample": nrm(ks[3], (DEC_BATCH, D), 1.0),
        "w_ada": nrm(ks[4], (L, D, 6 * D), 0.5 * D ** -0.5),
        "b_ada": nrm(ks[5], (L, 6 * D), 0.01),
        "w_in": nrm(ks[6], (L, D, D_IN), D ** -0.5),
        "pool_w": nrm(ks[7], (L, N_POOL_GROUPS, A_GROUP, A_GROUP), A_GROUP ** -0.5),
        "pool_scale": 1.0 + nrm(ks[8], (L, D_A), 0.1),
        "sgu_ln_g": 1.0 + nrm(ks[9], (L, D_B), 0.1),
        "sgu_ln_b": nrm(ks[10], (L, D_B), 0.01),
        "sgu_w": nrm(ks[11], (L, SGU_HEADS, CHUNK, CHUNK), 0.5 * CHUNK ** -0.5),
        "sgu_b": 1.0 + nrm(ks[12], (L, SGU_HEADS, CHUNK), 0.1),
        "w_up_a": nrm(ks[13], (L, D_A, D), D_A ** -0.5),
        "w_up_b": nrm(ks[14], (L, D_B, D), D_B ** -0.5),
        "w_out": nrm(ks[15], (L, D, D), BETA * D ** -0.5),
        "ln1_g": 1.0 + nrm(ks[16], (L, D), 0.1),
        "ln1_b": nrm(ks[17], (L, D), 0.01),
        "peer_wq": nrm(ks[18], (L, D, PEER_HEADS * PEER_DQ), D ** -0.5),
        "peer_keys": nrm(ks[19], (L, PEER_HEADS, 2, N_KEYS, PEER_DHALF), PEER_DHALF ** -0.5),
        "peer_u": nrm(ks[20], (L, N_EXPERTS, D), D ** -0.5),
        "peer_v": nrm(ks[21], (L, N_EXPERTS, D), BETA),
        "ln2_g": 1.0 + nrm(ks[22], (L, D), 0.1),
        "ln2_b": nrm(ks[23], (L, D), 0.01),
    }


def reference(x_prompt, x_sample, c_prompt, c_sample, w_ada, b_ada, w_in, pool_w, pool_scale,
              sgu_ln_g, sgu_ln_b, sgu_w, sgu_b, w_up_a, w_up_b, w_out, ln1_g, ln1_b,
              peer_wq, peer_keys, peer_u, peer_v, ln2_g, ln2_b):
    y_prompt = trunk(x_prompt, c_prompt, w_ada, b_ada, w_in, pool_w, pool_scale, sgu_ln_g,
                     sgu_ln_b, sgu_w, sgu_b, w_up_a, w_up_b, w_out, ln1_g, ln1_b, peer_wq,
                     peer_keys, peer_u, peer_v, ln2_g, ln2_b)
    y_sample = trunk(x_sample, c_sample, w_ada, b_ada, w_in, pool_w, pool_scale, sgu_ln_g,
                     sgu_ln_b, sgu_w, sgu_b, w_up_a, w_up_b, w_out, ln1_g, ln1_b, peer_wq,
                     peer_keys, peer_u, peer_v, ln2_g, ln2_b)
    return (y_prompt, y_sample)
```

```python
import functools
import math

import jax
import jax.numpy as jnp
from jax import lax
from jax.experimental import pallas as pl
from jax.experimental.pallas import tpu as pltpu

F32 = jnp.float32
BF16 = jnp.bfloat16

D_MODEL = 2048
D_A = 1024
D_B = 1024
POOL_HALF = (1, 2, 4, 8)
A_GROUP = 256
CHUNK = 128
SGU_HEADS = 8
PEER_HEADS = 8
N_KEYS = 128
N_EXPERTS = N_KEYS * N_KEYS
TOPK = 16
DEPTH = 1
ALPHA = (2.0 * DEPTH) ** 0.25
LN_EPS = 1e-5
HALO = 8
LANES = 128
NEG_INF = float("-inf")
VMEM_LIMIT = 56 * 1024 * 1024

_CAND = tuple((k1, k2) for k1 in range(TOPK) for k2 in range(TOPK) if (k1 + 1) * (k2 + 1) <= TOPK)


def _gelu(x):
    c = math.sqrt(2.0 / math.pi)
    return x * (0.5 * (1.0 + jnp.tanh(c * (x + 0.044715 * (x * x * x)))))


def _sigmoid(x):
    return 1.0 / (1.0 + jnp.exp(-x))


def _norm0(x):
    mu = jnp.mean(x, axis=-1, keepdims=True)
    xc = x - mu
    var = jnp.mean(xc * xc, axis=-1, keepdims=True)
    return xc * lax.rsqrt(var + LN_EPS)


def _params(sem):
    return pltpu.CompilerParams(dimension_semantics=sem, vmem_limit_bytes=VMEM_LIMIT)


def _ada_kernel(c_ref, w_ref, b_ref, o_ref):
    c = c_ref[...]
    sc = c * _sigmoid(c)
    o_ref[...] = jnp.dot(sc.astype(BF16), w_ref[...].astype(BF16),
                         preferred_element_type=F32) + b_ref[...]


def _ada(c8, w_ada, b_ada):
    n = w_ada.shape[1]
    tn = 1024
    return pl.pallas_call(
        _ada_kernel,
        grid=(n // tn,),
        in_specs=[pl.BlockSpec((8, D_MODEL), lambda j: (0, 0)),
                  pl.BlockSpec((D_MODEL, tn), lambda j: (0, j)),
                  pl.BlockSpec((1, tn), lambda j: (0, j))],
        out_specs=pl.BlockSpec((8, tn), lambda j: (0, j)),
        out_shape=jax.ShapeDtypeStruct((8, n), F32),
        compiler_params=_params(("arbitrary",)),
        name="ada",
    )(c8, w_ada, b_ada)


def _tcast_kernel(x_ref, o_ref):
    o_ref[...] = x_ref[...].T.astype(BF16)


def _transpose_cast(x, tr=512):
    r, c = x.shape
    return pl.pallas_call(
        _tcast_kernel,
        grid=(r // tr,),
        in_specs=[pl.BlockSpec((tr, c), lambda i: (i, 0))],
        out_specs=pl.BlockSpec((c, tr), lambda i: (0, i)),
        out_shape=jax.ShapeDtypeStruct((c, r), BF16),
        compiler_params=_params(("arbitrary",)),
        name="transpose_cast",
    )(x)


def _inproj_kernel(x_ref, shift_ref, scale_ref, w_ref, lng_ref, lnb_ref, ws_ref, bs_ref,
                   a_ref, ob_ref, sg_ref, h_scr, gu_scr):
    j = pl.program_id(2)
    tt = x_ref.shape[1]

    @pl.when(j == 0)
    def _():
        h = _norm0(x_ref[0]) * (1.0 + scale_ref[0]) + shift_ref[0]
        h_scr[...] = h.astype(BF16)

    p = jnp.dot(h_scr[...], w_ref[...], preferred_element_type=F32)

    @pl.when(j == 0)
    def _():
        a_ref[0] = p

    @pl.when(j == 1)
    def _():
        gu_scr[...] = _gelu(p).astype(BF16)

    @pl.when(j == 2)
    def _():
        vn = (_norm0(_gelu(p)) * lng_ref[...] + lnb_ref[...]).astype(BF16)
        for c in range(tt // CHUNK):
            rows = slice(c * CHUNK, (c + 1) * CHUNK)
            for hd in range(SGU_HEADS):
                cols = slice(hd * CHUNK, (hd + 1) * CHUNK)
                mixed = jnp.dot(ws_ref[hd], vn[rows, cols],
                                preferred_element_type=F32) + bs_ref[hd]
                ob_ref[0, rows, cols] = (gu_scr[rows, cols].astype(F32) * mixed).astype(BF16)

    @pl.when(j >= 3)
    def _():
        sg_ref[0] = _sigmoid(p).astype(BF16)


def _inproj(x, shift1, scale1, w_in, lng, lnb, ws, bs_full, tt):
    b, s, d = x.shape
    tn = 1024
    nj = w_in.shape[1] // tn
    return pl.pallas_call(
        _inproj_kernel,
        grid=(b, s // tt, nj),
        in_specs=[pl.BlockSpec((1, tt, d), lambda bi, i, j: (bi, i, 0)),
                  pl.BlockSpec((1, 1, d), lambda bi, i, j: (bi, 0, 0)),
                  pl.BlockSpec((1, 1, d), lambda bi, i, j: (bi, 0, 0)),
                  pl.BlockSpec((d, tn), lambda bi, i, j: (0, j)),
                  pl.BlockSpec((1, D_B), lambda bi, i, j: (0, 0)),
                  pl.BlockSpec((1, D_B), lambda bi, i, j: (0, 0)),
                  pl.BlockSpec((SGU_HEADS, CHUNK, CHUNK), lambda bi, i, j: (0, 0, 0)),
                  pl.BlockSpec((SGU_HEADS, CHUNK, CHUNK), lambda bi, i, j: (0, 0, 0))],
        out_specs=[pl.BlockSpec((1, tt, D_A), lambda bi, i, j: (bi, i, 0)),
                   pl.BlockSpec((1, tt, D_B), lambda bi, i, j: (bi, i, 0)),
                   pl.BlockSpec((1, tt, tn), lambda bi, i, j: (bi, i, jnp.maximum(j - 3, 0)))],
        out_shape=[jax.ShapeDtypeStruct((b, s, D_A), F32),
                   jax.ShapeDtypeStruct((b, s, D_B), BF16),
                   jax.ShapeDtypeStruct((b, s, 2 * d), BF16)],
        scratch_shapes=[pltpu.VMEM((tt, d), BF16), pltpu.VMEM((tt, D_B), BF16)],
        compiler_params=_params(("arbitrary", "arbitrary", "arbitrary")),
        name="inproj",
    )(x, shift1, scale1, w_in, lng, lnb, ws, bs_full)


def _merge_kernel(x_ref, a_ref, ap_ref, an_ref, ob_ref, sg_ref, pw_ref, ps_ref, wua_ref, wub_ref,
                  wo_ref, gate1_ref, g1_ref, b1_ref, shift2_ref, scale2_ref,
                  x1_ref, h2t_ref, ext_scr, *, seq_len):
    i = pl.program_id(1)
    ni = pl.num_programs(1)
    tt = x_ref.shape[1]
    d = x_ref.shape[2]

    ext_scr[0:HALO, :] = jnp.where(i > 0, ap_ref[0], 0.0)
    ext_scr[HALO:HALO + tt, :] = a_ref[0]
    ext_scr[HALO + tt:HALO + tt + HALO, :] = jnp.where(i < ni - 1, an_ref[0], 0.0)

    pos = i * tt + lax.broadcasted_iota(jnp.int32, (tt, A_GROUP), 0)
    ya = jnp.zeros((tt, d), F32)
    for g, hw in enumerate(POOL_HALF):
        cols = slice(g * A_GROUP, (g + 1) * A_GROUP)
        win = ext_scr[HALO - hw:HALO - hw + tt, cols]
        for off in range(-hw + 1, hw):
            win = win + ext_scr[HALO + off:HALO + off + tt, cols]
        cnt = (jnp.minimum(pos + hw, seq_len) - jnp.maximum(pos - hw, 0)).astype(F32)
        pooled = win / cnt - ext_scr[HALO:HALO + tt, cols]
        mixed = jnp.dot(pooled.astype(BF16), pw_ref[g], preferred_element_type=F32) * ps_ref[:, cols]
        ya = ya + jnp.dot(mixed.astype(BF16), wua_ref[cols, :], preferred_element_type=F32)

    yb = jnp.dot(ob_ref[0], wub_ref[...], preferred_element_type=F32)
    merged = sg_ref[0, :, 0:d].astype(F32) * ya + sg_ref[0, :, d:2 * d].astype(F32) * yb
    y = jnp.dot(merged.astype(BF16), wo_ref[...], preferred_element_type=F32)
    x1 = _norm0(ALPHA * x_ref[0] + gate1_ref[0] * y) * g1_ref[...] + b1_ref[...]
    x1_ref[0] = x1
    h2 = _norm0(x1) * (1.0 + scale2_ref[0]) + shift2_ref[0]
    h2t_ref[0] = h2.T.astype(BF16)


def _merge(x, a, ob, sg, pool_w, pool_scale, w_up_a, w_up_b, w_out, gate1, g1, b1, shift2, scale2, tt):
    b, s, d = x.shape
    nh = tt // HALO
    last = s // HALO - 1
    vec = lambda: pl.BlockSpec((1, 1, d), lambda bi, i: (bi, 0, 0))
    row = lambda n: pl.BlockSpec((1, n), lambda bi, i: (0, 0))
    return pl.pallas_call(
        functools.partial(_merge_kernel, seq_len=s),
        grid=(b, s // tt),
        in_specs=[pl.BlockSpec((1, tt, d), lambda bi, i: (bi, i, 0)),
                  pl.BlockSpec((1, tt, D_A), lambda bi, i: (bi, i, 0)),
                  pl.BlockSpec((1, HALO, D_A), lambda bi, i: (bi, jnp.maximum(i * nh - 1, 0), 0)),
                  pl.BlockSpec((1, HALO, D_A), lambda bi, i: (bi, jnp.minimum((i + 1) * nh, last), 0)),
                  pl.BlockSpec((1, tt, D_B), lambda bi, i: (bi, i, 0)),
                  pl.BlockSpec((1, tt, 2 * d), lambda bi, i: (bi, i, 0)),
                  pl.BlockSpec((len(POOL_HALF), A_GROUP, A_GROUP), lambda bi, i: (0, 0, 0)),
                  row(D_A),
                  pl.BlockSpec((D_A, d), lambda bi, i: (0, 0)),
                  pl.BlockSpec((D_B, d), lambda bi, i: (0, 0)),
                  pl.BlockSpec((d, d), lambda bi, i: (0, 0)),
                  vec(), row(d), row(d), vec(), vec()],
        out_specs=[pl.BlockSpec((1, tt, d), lambda bi, i: (bi, i, 0)),
                   pl.BlockSpec((1, d, tt), lambda bi, i: (bi, 0, i))],
        out_shape=[jax.ShapeDtypeStruct((b, s, d), F32),
                   jax.ShapeDtypeStruct((b, d, s), BF16)],
        scratch_shapes=[pltpu.VMEM((tt + 2 * HALO, D_A), F32)],
        compiler_params=_params(("arbitrary", "arbitrary")),
        name="merge",
    )(x, a, a, a, ob, sg, pool_w, pool_scale, w_up_a, w_up_b, w_out, gate1, g1, b1, shift2, scale2)


def _extract_top(work_ref, vals_ref, n_rounds):
    def body(k, carry):
        s = work_ref[...]
        m = jnp.max(s, axis=0)
        vals_ref[k] = m
        work_ref[...] = jnp.where(s == m[None], NEG_INF, s)
        return carry
    lax.fori_loop(0, n_rounds, body, 0)


def _route_kernel(h2t_ref, wqt_ref, kbd_ref, a_ref, thr_ref, s2_ref, b_ref,
                  s_scr, work_scr, v1_scr, v2_scr, cand_scr, cv_scr):
    tt = h2t_ref.shape[2]
    nk = N_KEYS
    half_rows = nk * PEER_HEADS
    qt = jnp.dot(wqt_ref[...], h2t_ref[0], preferred_element_type=F32).astype(BF16)
    s_all = jnp.dot(kbd_ref[...], qt, preferred_element_type=F32)

    for c in range(tt // LANES):
        lanes = slice(c * LANES, (c + 1) * LANES)
        s_scr[...] = s_all[:, lanes]
        work_scr[...] = s_scr[0:half_rows, :].reshape(nk, PEER_HEADS, LANES)
        _extract_top(work_scr, v1_scr, TOPK)
        work_scr[...] = s_scr[half_rows:2 * half_rows, :].reshape(nk, PEER_HEADS, LANES)
        _extract_top(work_scr, v2_scr, TOPK)

        for n, (k1, k2) in enumerate(_CAND):
            cand_scr[n] = v1_scr[k1] + v2_scr[k2]
        _extract_top(cand_scr, cv_scr, TOPK)
        tau = cv_scr[TOPK - 1]
        v1_top = v1_scr[0]
        v2_top = v2_scr[0]
        tau1 = v1_scr[TOPK - 1]
        tau2 = v2_scr[TOPK - 1]

        z = jnp.zeros_like(tau)
        for (k1, k2) in _CAND:
            sel = v2_scr[k2] >= (tau - v1_scr[k1])
            z = z + jnp.where(sel, jnp.exp(v1_scr[k1] - v1_top) * jnp.exp(v2_scr[k2] - v2_top), 0.0)
        inv_z = 1.0 / z

        s1 = s_scr[0:half_rows, :].reshape(nk, PEER_HEADS, LANES)
        a_ref[0, :, :, lanes] = jnp.where(s1 >= tau1[None], jnp.exp(s1 - v1_top[None]), 0.0) * inv_z[None]
        thr_ref[0, :, :, lanes] = tau[None] - s1

        for hd in range(PEER_HEADS):
            s2h = s_scr[pl.ds(half_rows + hd, nk, stride=PEER_HEADS), :]
            s2_ref[0, hd, :, lanes] = s2h
            b_ref[0, hd, :, lanes] = jnp.where(s2h >= tau2[hd:hd + 1, :],
                                               jnp.exp(s2h - v2_top[hd:hd + 1, :]), 0.0)


def _route(h2t, wqt, kbd, tt):
    b, d, s = h2t.shape
    nk, nh = N_KEYS, PEER_HEADS
    full = lambda: pl.BlockSpec((d, d), lambda bi, i: (0, 0))
    eh = lambda: pl.BlockSpec((1, nk, nh, tt), lambda bi, i: (bi, 0, 0, i))
    he = lambda: pl.BlockSpec((1, nh, nk, tt), lambda bi, i: (bi, 0, 0, i))
    return pl.pallas_call(
        _route_kernel,
        grid=(b, s // tt),
        in_specs=[pl.BlockSpec((1, d, tt), lambda bi, i: (bi, 0, i)), full(), full()],
        out_specs=[eh(), eh(), he(), he()],
        out_shape=[jax.ShapeDtypeStruct((b, nk, nh, s), F32),
                   jax.ShapeDtypeStruct((b, nk, nh, s), F32),
                   jax.ShapeDtypeStruct((b, nh, nk, s), F32),
                   jax.ShapeDtypeStruct((b, nh, nk, s), F32)],
        scratch_shapes=[pltpu.VMEM((2 * nk * nh, LANES), F32),
                        pltpu.VMEM((nk, nh, LANES), F32),
                        pltpu.VMEM((TOPK, nh, LANES), F32),
                        pltpu.VMEM((TOPK, nh, LANES), F32),
                        pltpu.VMEM((len(_CAND), nh, LANES), F32),
                        pltpu.VMEM((TOPK, nh, LANES), F32)],
        compiler_params=_params(("arbitrary", "arbitrary")),
        name="route",
    )(h2t, wqt, kbd)


def _experts_kernel(h2t_ref, u_ref, vt_ref, a_ref, thr_ref, s2_ref, b_ref, x1_ref, gate2_ref,
                    g2_ref, b2_ref, y_ref, acc_scr):
    e = pl.program_id(2)
    ne = pl.num_programs(2)
    te = u_ref.shape[0]

    @pl.when(e == 0)
    def _():
        acc_scr[...] = jnp.zeros_like(acc_scr)

    act = _gelu(jnp.dot(u_ref[...], h2t_ref[0], preferred_element_type=F32))
    parts = []
    for l in range(te // N_KEYS):
        w = None
        for hd in range(PEER_HEADS):
            thr = thr_ref[0, l, hd:hd + 1, :]
            a = a_ref[0, l, hd:hd + 1, :]
            term = jnp.where(s2_ref[0, hd] >= thr, b_ref[0, hd], 0.0) * a
            w = term if w is None else w + term
        parts.append((w * act[l * N_KEYS:(l + 1) * N_KEYS, :]).astype(BF16))
    xw = jnp.concatenate(parts, axis=0)
    acc_scr[...] += jnp.dot(vt_ref[...], xw, preferred_element_type=F32)

    @pl.when(e == ne - 1)
    def _():
        y2 = acc_scr[...].T
        y_ref[0] = _norm0(ALPHA * x1_ref[0] + gate2_ref[0] * y2) * g2_ref[...] + b2_ref[...]


def _experts(h2t, u_bf, vt_bf, a, thr, s2, bm, x1, gate2, g2, b2, tt, te):
    b, d, s = h2t.shape
    nk, nh = N_KEYS, PEER_HEADS
    nl = te // nk
    eh = lambda: pl.BlockSpec((1, nl, nh, tt), lambda bi, i, e: (bi, e, 0, i))
    he = lambda: pl.BlockSpec((1, nh, nk, tt), lambda bi, i, e: (bi, 0, 0, i))
    row = lambda: pl.BlockSpec((1, d), lambda bi, i, e: (0, 0))
    return pl.pallas_call(
        _experts_kernel,
        grid=(b, s // tt, N_EXPERTS // te),
        in_specs=[pl.BlockSpec((1, d, tt), lambda bi, i, e: (bi, 0, i)),
                  pl.BlockSpec((te, d), lambda bi, i, e: (e, 0)),
                  pl.BlockSpec((d, te), lambda bi, i, e: (0, e)),
                  eh(), eh(), he(), he(),
                  pl.BlockSpec((1, tt, d), lambda bi, i, e: (bi, i, 0)),
                  pl.BlockSpec((1, 1, d), lambda bi, i, e: (bi, 0, 0)),
                  row(), row()],
        out_specs=pl.BlockSpec((1, tt, d), lambda bi, i, e: (bi, i, 0)),
        out_shape=jax.ShapeDtypeStruct((b, s, d), F32),
        scratch_shapes=[pltpu.VMEM((d, tt), F32)],
        compiler_params=_params(("arbitrary", "arbitrary", "arbitrary")),
        name="experts",
    )(h2t, u_bf, vt_bf, a, thr, s2, bm, x1, gate2, g2, b2)


def _block_diag_keys(keys):
    nh, _, nk, c = keys.shape
    eye = jnp.eye(nh, dtype=keys.dtype)
    full = jnp.einsum("hknc,hg,kj->knhgjc", keys, eye, jnp.eye(2, dtype=keys.dtype))
    return full.reshape(2 * nk * nh, nh * 2 * c)


def _trunk(x, mod, wts, tiles):
    b, s, d = x.shape
    shift1, scale1, gate1, shift2, scale2, gate2 = [mod[:, k][:, None, :] for k in range(6)]
    t_in, t_merge, t_route, t_exp, t_e = tiles
    a, ob, sg = _inproj(x, shift1, scale1, wts["w_in"], wts["sgu_ln_g"], wts["sgu_ln_b"],
                        wts["sgu_w"], wts["sgu_b_full"], min(t_in, s))
    x1, h2t = _merge(x, a, ob, sg, wts["pool_w"], wts["pool_scale"], wts["w_up_a"], wts["w_up_b"],
                     wts["w_out"], gate1, wts["ln1_g"], wts["ln1_b"], shift2, scale2, min(t_merge, s))
    ra, rthr, rs2, rb = _route(h2t, wts["wqt"], wts["kbd"], min(t_route, s))
    return _experts(h2t, wts["u_bf"], wts["vt_bf"], ra, rthr, rs2, rb, x1, gate2,
                    wts["ln2_g"], wts["ln2_b"], min(t_exp, s), t_e)


def _prepare(w_ada, b_ada, w_in, pool_w, pool_scale, sgu_ln_g, sgu_ln_b, sgu_w, sgu_b, w_up_a, w_up_b,
             w_out, ln1_g, ln1_b, peer_wq, peer_keys, peer_u, peer_v, ln2_g, ln2_b):
    l = 0
    return {
        "w_in": w_in[l].astype(BF16),
        "pool_w": pool_w[l].astype(BF16),
        "pool_scale": pool_scale[l][None, :],
        "sgu_ln_g": sgu_ln_g[l][None, :],
        "sgu_ln_b": sgu_ln_b[l][None, :],
        "sgu_w": sgu_w[l].astype(BF16),
        "sgu_b_full": jnp.broadcast_to(sgu_b[l][:, :, None], (SGU_HEADS, CHUNK, CHUNK)),
        "w_up_a": w_up_a[l].astype(BF16),
        "w_up_b": w_up_b[l].astype(BF16),
        "w_out": w_out[l].astype(BF16),
        "ln1_g": ln1_g[l][None, :],
        "ln1_b": ln1_b[l][None, :],
        "wqt": _transpose_cast(peer_wq[l]),
        "kbd": _block_diag_keys(peer_keys[l]).astype(BF16),
        "u_bf": peer_u[l].astype(BF16),
        "vt_bf": _transpose_cast(peer_v[l]),
        "ln2_g": ln2_g[l][None, :],
        "ln2_b": ln2_b[l][None, :],
    }


_TILES = (512, 256, 256, 512, 512)


def kernel(x_prompt, x_sample, c_prompt, c_sample, w_ada, b_ada, w_in, pool_w, pool_scale, sgu_ln_g,
           sgu_ln_b, sgu_w, sgu_b, w_up_a, w_up_b, w_out, ln1_g, ln1_b, peer_wq, peer_keys, peer_u,
           peer_v, ln2_g, ln2_b):
    assert w_ada.shape[0] == DEPTH == 1
    bp = x_prompt.shape[0]
    bs = x_sample.shape[0]
    c_all = jnp.concatenate([c_prompt, c_sample], axis=0)
    c8 = jnp.pad(c_all, ((0, 8 - bp - bs), (0, 0)))
    mod = _ada(c8, w_ada[0], b_ada[0][None, :]).reshape(8, 6, D_MODEL)
    wts = _prepare(w_ada, b_ada, w_in, pool_w, pool_scale, sgu_ln_g, sgu_ln_b, sgu_w, sgu_b, w_up_a,
                   w_up_b, w_out, ln1_g, ln1_b, peer_wq, peer_keys, peer_u, peer_v, ln2_g, ln2_b)
    y_prompt = _trunk(x_prompt, mod[:bp], wts, _TILES)
    y_sample = _trunk(x_sample, mod[bp:bp + bs], wts, _TILES)
    return (y_prompt, y_sample)
```

```python
import functools
import math

import jax
import jax.numpy as jnp
from jax import lax
from jax.experimental import pallas as pl
from jax.experimental.pallas import tpu as pltpu

F32 = jnp.float32
BF16 = jnp.bfloat16

D_MODEL = 2048
D_A = 1024
D_B = 1024
POOL_HALF = (1, 2, 4, 8)
A_GROUP = 256
CHUNK = 128
SGU_HEADS = 8
PEER_HEADS = 8
N_KEYS = 128
N_EXPERTS = N_KEYS * N_KEYS
TOPK = 16
EXPERT_SUB = 512
DEPTH = 1
ALPHA = (2.0 * DEPTH) ** 0.25
LN_EPS = 1e-5
HALO = 8
LANES = 128
MXU_WIDTH = 256
PACK_ROWS = 16
NEG_INF = float("-inf")
VMEM_LIMIT = 60 * 1024 * 1024

_CAND = tuple((k1, k2) for k1 in range(TOPK) for k2 in range(TOPK) if (k1 + 1) * (k2 + 1) <= TOPK)


def _gelu(x):
    c = math.sqrt(2.0 / math.pi)
    return x * (0.5 * (1.0 + jnp.tanh(c * (x + 0.044715 * (x * x * x)))))


def _sigmoid(x):
    return 1.0 / (1.0 + jnp.exp(-x))


def _norm0(x):
    mu = jnp.mean(x, axis=-1, keepdims=True)
    xc = x - mu
    var = jnp.mean(xc * xc, axis=-1, keepdims=True)
    return xc * lax.rsqrt(var + LN_EPS)


def _params(sem, flags=None):
    return pltpu.CompilerParams(dimension_semantics=sem, vmem_limit_bytes=VMEM_LIMIT, flags=flags)


def _ada_kernel(c_ref, w_ref, b_ref, o_ref):
    c = c_ref[...]
    sc = c * _sigmoid(c)
    o_ref[...] = jnp.dot(sc.astype(BF16), w_ref[...].astype(BF16),
                         preferred_element_type=F32) + b_ref[...]


def _ada(c8, w_ada, b_ada):
    n = w_ada.shape[1]
    tn = 1024
    return pl.pallas_call(
        _ada_kernel,
        grid=(n // tn,),
        in_specs=[pl.BlockSpec((8, D_MODEL), lambda j: (0, 0)),
                  pl.BlockSpec((D_MODEL, tn), lambda j: (0, j)),
                  pl.BlockSpec((1, tn), lambda j: (0, j))],
        out_specs=pl.BlockSpec((8, tn), lambda j: (0, j)),
        out_shape=jax.ShapeDtypeStruct((8, n), F32),
        compiler_params=_params(("arbitrary",)),
        name="ada",
    )(c8, w_ada, b_ada)


def _tcast_kernel(x_ref, o_ref):
    o_ref[...] = x_ref[...].T.astype(BF16)


def _transpose_cast(x, tr=512):
    r, c = x.shape
    return pl.pallas_call(
        _tcast_kernel,
        grid=(r // tr,),
        in_specs=[pl.BlockSpec((tr, c), lambda i: (i, 0))],
        out_specs=pl.BlockSpec((c, tr), lambda i: (0, i)),
        out_shape=jax.ShapeDtypeStruct((c, r), BF16),
        compiler_params=_params(("arbitrary",)),
        name="transpose_cast",
    )(x)


def _inproj_kernel(x_ref, shift_ref, scale_ref, w_ref, lng_ref, lnb_ref, ws_ref, bs_ref,
                   a_ref, ob_ref, sg_ref, h_scr, gu_scr):
    j = pl.program_id(2)
    tt = x_ref.shape[1]

    @pl.when(j == 0)
    def _():
        h = _norm0(x_ref[0]) * (1.0 + scale_ref[0]) + shift_ref[0]
        h_scr[...] = h.astype(BF16)

    p = jnp.dot(h_scr[...], w_ref[...], preferred_element_type=F32)

    @pl.when(j == 0)
    def _():
        a_ref[0] = p

    @pl.when(j == 1)
    def _():
        gu_scr[...] = _gelu(p).astype(BF16)

    @pl.when(j == 2)
    def _():
        vn = (_norm0(_gelu(p)) * lng_ref[...] + lnb_ref[...]).astype(BF16)
        for c in range(tt // CHUNK):
            rows = slice(c * CHUNK, (c + 1) * CHUNK)
            for hd in range(SGU_HEADS):
                cols = slice(hd * CHUNK, (hd + 1) * CHUNK)
                mixed = jnp.dot(ws_ref[hd], vn[rows, cols],
                                preferred_element_type=F32) + bs_ref[hd]
                ob_ref[0, rows, cols] = (gu_scr[rows, cols].astype(F32) * mixed).astype(BF16)

    @pl.when(j >= 3)
    def _():
        sg_ref[0] = _sigmoid(p).astype(BF16)


def _inproj(x, shift1, scale1, w_in, lng, lnb, ws, bs_full, tt):
    b, s, d = x.shape
    tn = 1024
    nj = w_in.shape[1] // tn
    return pl.pallas_call(
        _inproj_kernel,
        grid=(b, s // tt, nj),
        in_specs=[pl.BlockSpec((1, tt, d), lambda bi, i, j: (bi, i, 0)),
                  pl.BlockSpec((1, 1, d), lambda bi, i, j: (bi, 0, 0)),
                  pl.BlockSpec((1, 1, d), lambda bi, i, j: (bi, 0, 0)),
                  pl.BlockSpec((d, tn), lambda bi, i, j: (0, j)),
                  pl.BlockSpec((1, D_B), lambda bi, i, j: (0, 0)),
                  pl.BlockSpec((1, D_B), lambda bi, i, j: (0, 0)),
                  pl.BlockSpec((SGU_HEADS, CHUNK, CHUNK), lambda bi, i, j: (0, 0, 0)),
                  pl.BlockSpec((SGU_HEADS, CHUNK, CHUNK), lambda bi, i, j: (0, 0, 0))],
        out_specs=[pl.BlockSpec((1, tt, D_A), lambda bi, i, j: (bi, i, 0)),
                   pl.BlockSpec((1, tt, D_B), lambda bi, i, j: (bi, i, 0)),
                   pl.BlockSpec((1, tt, tn), lambda bi, i, j: (bi, i, jnp.maximum(j - 3, 0)))],
        out_shape=[jax.ShapeDtypeStruct((b, s, D_A), F32),
                   jax.ShapeDtypeStruct((b, s, D_B), BF16),
                   jax.ShapeDtypeStruct((b, s, 2 * d), BF16)],
        scratch_shapes=[pltpu.VMEM((tt, d), BF16), pltpu.VMEM((tt, D_B), BF16)],
        compiler_params=_params(("arbitrary", "arbitrary", "arbitrary")),
        name="inproj",
    )(x, shift1, scale1, w_in, lng, lnb, ws, bs_full)


def _merge_kernel(x_ref, a_ref, ap_ref, an_ref, ob_ref, sg_ref, pw_ref, ps_ref, wua_ref, wub_ref,
                  wo_ref, gate1_ref, g1_ref, b1_ref, shift2_ref, scale2_ref,
                  x1_ref, h2t_ref, ext_scr, *, seq_len):
    i = pl.program_id(1)
    ni = pl.num_programs(1)
    tt = x_ref.shape[1]
    d = x_ref.shape[2]

    ext_scr[0:HALO, :] = jnp.where(i > 0, ap_ref[0], 0.0)
    ext_scr[HALO:HALO + tt, :] = a_ref[0]
    ext_scr[HALO + tt:HALO + tt + HALO, :] = jnp.where(i < ni - 1, an_ref[0], 0.0)

    pos = i * tt + lax.broadcasted_iota(jnp.int32, (tt, A_GROUP), 0)
    ya = jnp.zeros((tt, d), F32)
    for g, hw in enumerate(POOL_HALF):
        cols = slice(g * A_GROUP, (g + 1) * A_GROUP)
        win = ext_scr[HALO - hw:HALO - hw + tt, cols]
        for off in range(-hw + 1, hw):
            win = win + ext_scr[HALO + off:HALO + off + tt, cols]
        cnt = (jnp.minimum(pos + hw, seq_len) - jnp.maximum(pos - hw, 0)).astype(F32)
        pooled = win / cnt - ext_scr[HALO:HALO + tt, cols]
        mixed = jnp.dot(pooled.astype(BF16), pw_ref[g], preferred_element_type=F32) * ps_ref[:, cols]
        ya = ya + jnp.dot(mixed.astype(BF16), wua_ref[cols, :], preferred_element_type=F32)

    yb = jnp.dot(ob_ref[0], wub_ref[...], preferred_element_type=F32)
    merged = sg_ref[0, :, 0:d].astype(F32) * ya + sg_ref[0, :, d:2 * d].astype(F32) * yb
    y = jnp.dot(merged.astype(BF16), wo_ref[...], preferred_element_type=F32)
    x1 = _norm0(ALPHA * x_ref[0] + gate1_ref[0] * y) * g1_ref[...] + b1_ref[...]
    x1_ref[0] = x1
    h2 = _norm0(x1) * (1.0 + scale2_ref[0]) + shift2_ref[0]
    h2t_ref[0] = h2.T.astype(BF16)


def _merge(x, a, ob, sg, pool_w, pool_scale, w_up_a, w_up_b, w_out, gate1, g1, b1, shift2, scale2, tt):
    b, s, d = x.shape
    nh = tt // HALO
    last = s // HALO - 1
    vec = lambda: pl.BlockSpec((1, 1, d), lambda bi, i: (bi, 0, 0))
    row = lambda n: pl.BlockSpec((1, n), lambda bi, i: (0, 0))
    return pl.pallas_call(
        functools.partial(_merge_kernel, seq_len=s),
        grid=(b, s // tt),
        in_specs=[pl.BlockSpec((1, tt, d), lambda bi, i: (bi, i, 0)),
                  pl.BlockSpec((1, tt, D_A), lambda bi, i: (bi, i, 0)),
                  pl.BlockSpec((1, HALO, D_A), lambda bi, i: (bi, jnp.maximum(i * nh - 1, 0), 0)),
                  pl.BlockSpec((1, HALO, D_A), lambda bi, i: (bi, jnp.minimum((i + 1) * nh, last), 0)),
                  pl.BlockSpec((1, tt, D_B), lambda bi, i: (bi, i, 0)),
                  pl.BlockSpec((1, tt, 2 * d), lambda bi, i: (bi, i, 0)),
                  pl.BlockSpec((len(POOL_HALF), A_GROUP, A_GROUP), lambda bi, i: (0, 0, 0)),
                  row(D_A),
                  pl.BlockSpec((D_A, d), lambda bi, i: (0, 0)),
                  pl.BlockSpec((D_B, d), lambda bi, i: (0, 0)),
                  pl.BlockSpec((d, d), lambda bi, i: (0, 0)),
                  vec(), row(d), row(d), vec(), vec()],
        out_specs=[pl.BlockSpec((1, tt, d), lambda bi, i: (bi, i, 0)),
                   pl.BlockSpec((1, d, tt), lambda bi, i: (bi, 0, i))],
        out_shape=[jax.ShapeDtypeStruct((b, s, d), F32),
                   jax.ShapeDtypeStruct((b, d, s), BF16)],
        scratch_shapes=[pltpu.VMEM((tt + 2 * HALO, D_A), F32)],
        compiler_params=_params(("arbitrary", "arbitrary")),
        name="merge",
    )(x, a, a, a, ob, sg, pool_w, pool_scale, w_up_a, w_up_b, w_out, gate1, g1, b1, shift2, scale2)


def _bitonic_merge(xs, descending):
    n = len(xs)
    if n == 1:
        return xs
    h = n // 2
    big = [jnp.maximum(xs[i], xs[i + h]) for i in range(h)]
    small = [jnp.minimum(xs[i], xs[i + h]) for i in range(h)]
    first, second = (big, small) if descending else (small, big)
    return _bitonic_merge(first, descending) + _bitonic_merge(second, descending)


def _bitonic_sort(xs, descending=True):
    n = len(xs)
    if n == 1:
        return xs
    h = n // 2
    return _bitonic_merge(_bitonic_sort(xs[:h], True) + _bitonic_sort(xs[h:], False), descending)


def _merge_top(a, b):
    n = len(a)
    return _bitonic_merge([jnp.maximum(a[i], b[n - 1 - i]) for i in range(n)], True)


def _top_sorted(xs, k):
    groups = [_bitonic_sort(xs[g:g + k]) for g in range(0, len(xs), k)]
    while len(groups) > 1:
        groups = [_merge_top(groups[g], groups[g + 1]) for g in range(0, len(groups), 2)]
    return groups[0]


def _route_kernel(h2t_ref, wqt_ref, kbd_ref, a_ref, thr_ref, b_ref, s2_scr):
    tt = h2t_ref.shape[2]
    nk, nh = N_KEYS, PEER_HEADS
    half_rows = nk * nh
    qt = jnp.dot(wqt_ref[...], h2t_ref[0], preferred_element_type=F32).astype(BF16)
    s_all = jnp.dot(kbd_ref[...], qt, preferred_element_type=F32)
    neg = jnp.full((nh, LANES), NEG_INF, F32)

    for c in range(tt // LANES):
        lanes = slice(c * LANES, (c + 1) * LANES)
        s1 = [s_all[n * nh:(n + 1) * nh, lanes] for n in range(nk)]
        s2 = [s_all[half_rows + n * nh:half_rows + (n + 1) * nh, lanes] for n in range(nk)]
        v1 = _top_sorted(s1, TOPK)
        v2 = _top_sorted(s2, TOPK)

        row = lambda k1: [v1[k1] + v2[k2] for k2 in range(TOPK // (k1 + 1))]
        l0 = row(0)
        l1 = row(1) + [neg] * 8
        l2 = _bitonic_sort(row(2) + row(3) + row(4) + [neg] * 4)
        l3 = _bitonic_sort(sum([row(k1) for k1 in range(5, TOPK)], []) + [neg] * 2)
        tau = _merge_top(_merge_top(l0, l1), _merge_top(l2, l3))[TOPK - 1]

        t0 = tau - v2[0]
        e1v = [jnp.exp(v - v1[0]) for v in v1]
        e2v = [jnp.exp(v - v2[0]) for v in v2]
        thrv = [jnp.exp(t0 - v) for v in v1]
        z = jnp.zeros_like(tau)
        for (k1, k2) in _CAND:
            z = z + jnp.where(e2v[k2] >= thrv[k1], e1v[k1] * e2v[k2], 0.0)
        inv_z = 1.0 / z

        for n in range(nk):
            a_ref[0, n, :, lanes] = jnp.where(s1[n] >= v1[TOPK - 1], jnp.exp(s1[n] - v1[0]), 0.0) * inv_z
            thr_ref[0, n, :, lanes] = jnp.exp(t0 - s1[n])

        s2_scr[...] = s_all[half_rows:2 * half_rows, lanes]
        tau2 = v2[TOPK - 1]
        for hd in range(nh):
            s2h = s2_scr[pl.ds(hd, nk, stride=nh), :]
            b_ref[0, hd, :, lanes] = jnp.where(s2h >= tau2[hd:hd + 1, :],
                                               jnp.exp(s2h - v2[0][hd:hd + 1, :]), 0.0)


def _route(h2t, wqt, kbd, tt):
    b, d, s = h2t.shape
    nk, nh = N_KEYS, PEER_HEADS
    full = lambda: pl.BlockSpec((d, d), lambda bi, i: (0, 0), pipeline_mode=pl.Buffered(1))
    eh = lambda: pl.BlockSpec((1, nk, nh, tt), lambda bi, i: (bi, 0, 0, i))
    he = lambda: pl.BlockSpec((1, nh, nk, tt), lambda bi, i: (bi, 0, 0, i))
    return pl.pallas_call(
        _route_kernel,
        grid=(b, s // tt),
        in_specs=[pl.BlockSpec((1, d, tt), lambda bi, i: (bi, 0, i)), full(), full()],
        out_specs=[eh(), eh(), he()],
        out_shape=[jax.ShapeDtypeStruct((b, nk, nh, s), F32),
                   jax.ShapeDtypeStruct((b, nk, nh, s), F32),
                   jax.ShapeDtypeStruct((b, nh, nk, s), F32)],
        scratch_shapes=[pltpu.VMEM((nk * nh, LANES), F32)],
        compiler_params=_params(("arbitrary", "arbitrary")),
        name="route",
    )(h2t, wqt, kbd)


def _experts_step(h2t_ref, u_ref, vt_ref, a_ref, thr_ref, b_ref, acc_scr, xw_scr, hu_next, hu_cur):
    te = u_ref.shape[0]
    tt = h2t_ref.shape[2]
    nsub = te // EXPERT_SUB
    for sub in range(nsub):
        rows = slice(sub * EXPERT_SUB, (sub + 1) * EXPERT_SUB)
        for half in range(tt // MXU_WIDTH):
            cols = slice(half * MXU_WIDTH, (half + 1) * MXU_WIDTH)
            hu_next[rows, cols] = jnp.dot(u_ref[rows, :], h2t_ref[0, :, cols], preferred_element_type=F32)
            for l in range(EXPERT_SUB // N_KEYS):
                e1 = sub * (EXPERT_SUB // N_KEYS) + l
                for c in range(MXU_WIDTH // LANES):
                    lo = half * MXU_WIDTH + c * LANES
                    lanes = slice(lo, lo + LANES)
                    thr = [jnp.broadcast_to(thr_ref[0, e1, hd:hd + 1, lanes], (PACK_ROWS, LANES))
                           for hd in range(PEER_HEADS)]
                    a = [jnp.broadcast_to(a_ref[0, e1, hd:hd + 1, lanes], (PACK_ROWS, LANES))
                         for hd in range(PEER_HEADS)]
                    for r in range(N_KEYS // PACK_ROWS):
                        k2 = slice(r * PACK_ROWS, (r + 1) * PACK_ROWS)
                        w = None
                        for hd in range(PEER_HEADS):
                            bv = b_ref[0, hd, k2, lanes]
                            term = jnp.where(bv >= thr[hd], bv, 0.0) * a[hd]
                            w = term if w is None else w + term
                        k2s = slice(e1 * N_KEYS + r * PACK_ROWS, e1 * N_KEYS + (r + 1) * PACK_ROWS)
                        xw_scr[k2s, lanes] = (w * _gelu(hu_cur[k2s, lanes])).astype(BF16)
            acc_scr[:, cols] += jnp.dot(vt_ref[:, rows], xw_scr[rows, cols], preferred_element_type=F32)


def _experts_kernel(h2t_ref, u_ref, vt_ref, a_ref, thr_ref, b_ref, x1_ref, gate2_ref,
                    g2_ref, b2_ref, y_ref, acc_scr, hu_even, hu_odd, xw_scr, *, ne):
    s = pl.program_id(1)

    @pl.when(s == 0)
    def _():
        acc_scr[...] = jnp.zeros_like(acc_scr)
        hu_odd[...] = jnp.zeros_like(hu_odd)

    step = functools.partial(_experts_step, h2t_ref, u_ref, vt_ref, a_ref, thr_ref, b_ref,
                             acc_scr, xw_scr)

    @pl.when(s % 2 == 0)
    def _():
        step(hu_even, hu_odd)

    @pl.when(s % 2 == 1)
    def _():
        step(hu_odd, hu_even)

    @pl.when((s >= 1) & ((s - 1) % ne == ne - 1))
    def _():
        y2 = acc_scr[...].T
        y_ref[0] = _norm0(ALPHA * x1_ref[0] + gate2_ref[0] * y2) * g2_ref[...] + b2_ref[...]
        acc_scr[...] = jnp.zeros_like(acc_scr)


def _experts(h2t, u_bf, vt_bf, a, thr, bm, x1, gate2, g2, b2, tt, te):
    b, d, s = h2t.shape
    nk, nh = N_KEYS, PEER_HEADS
    nl = te // nk
    ni = s // tt
    ne = N_EXPERTS // te
    nxt_i = lambda st: jnp.minimum(st // ne, ni - 1)
    cur = lambda st: jnp.maximum(st - 1, 0)
    eh = lambda: pl.BlockSpec((1, nl, nh, tt), lambda bi, st: (bi, cur(st) % ne, 0, cur(st) // ne))
    he = lambda: pl.BlockSpec((1, nh, nk, tt), lambda bi, st: (bi, 0, 0, cur(st) // ne))
    row = lambda: pl.BlockSpec((1, d), lambda bi, st: (0, 0))
    return pl.pallas_call(
        functools.partial(_experts_kernel, ne=ne),
        grid=(b, ni * ne + 1),
        in_specs=[pl.BlockSpec((1, d, tt), lambda bi, st: (bi, 0, nxt_i(st))),
                  pl.BlockSpec((te, d), lambda bi, st: (st % ne, 0)),
                  pl.BlockSpec((d, te), lambda bi, st: (0, cur(st) % ne)),
                  eh(), eh(), he(),
                  pl.BlockSpec((1, tt, d), lambda bi, st: (bi, cur(st) // ne, 0)),
                  pl.BlockSpec((1, 1, d), lambda bi, st: (bi, 0, 0)),
                  row(), row()],
        out_specs=pl.BlockSpec((1, tt, d), lambda bi, st: (bi, cur(st) // ne, 0)),
        out_shape=jax.ShapeDtypeStruct((b, s, d), F32),
        scratch_shapes=[pltpu.VMEM((d, tt), F32),
                        pltpu.VMEM((te, tt), F32),
                        pltpu.VMEM((te, tt), F32),
                        pltpu.VMEM((te, tt), BF16)],
        compiler_params=_params(("arbitrary", "arbitrary")),
        name="experts",
    )(h2t, u_bf, vt_bf, a, thr, bm, x1, gate2, g2, b2)


def _block_diag_keys(keys):
    nh, _, nk, c = keys.shape
    eye = jnp.eye(nh, dtype=keys.dtype)
    full = jnp.einsum("hknc,hg,kj->knhgjc", keys, eye, jnp.eye(2, dtype=keys.dtype))
    return full.reshape(2 * nk * nh, nh * 2 * c)


def _trunk(x, mod, wts, tiles):
    b, s, d = x.shape
    shift1, scale1, gate1, shift2, scale2, gate2 = [mod[:, k][:, None, :] for k in range(6)]
    t_in, t_merge, t_route, t_exp, t_e = tiles
    a, ob, sg = _inproj(x, shift1, scale1, wts["w_in"], wts["sgu_ln_g"], wts["sgu_ln_b"],
                        wts["sgu_w"], wts["sgu_b_full"], min(t_in, s))
    x1, h2t = _merge(x, a, ob, sg, wts["pool_w"], wts["pool_scale"], wts["w_up_a"], wts["w_up_b"],
                     wts["w_out"], gate1, wts["ln1_g"], wts["ln1_b"], shift2, scale2, min(t_merge, s))
    ra, rthr, rb = _route(h2t, wts["wqt"], wts["kbd"], min(t_route, s))
    return _experts(h2t, wts["u_bf"], wts["vt_bf"], ra, rthr, rb, x1, gate2,
                    wts["ln2_g"], wts["ln2_b"], min(t_exp, s), t_e)


def _prepare(w_ada, b_ada, w_in, pool_w, pool_scale, sgu_ln_g, sgu_ln_b, sgu_w, sgu_b, w_up_a, w_up_b,
             w_out, ln1_g, ln1_b, peer_wq, peer_keys, peer_u, peer_v, ln2_g, ln2_b):
    l = 0
    return {
        "w_in": w_in[l].astype(BF16),
        "pool_w": pool_w[l].astype(BF16),
        "pool_scale": pool_scale[l][None, :],
        "sgu_ln_g": sgu_ln_g[l][None, :],
        "sgu_ln_b": sgu_ln_b[l][None, :],
        "sgu_w": sgu_w[l].astype(BF16),
        "sgu_b_full": jnp.broadcast_to(sgu_b[l][:, :, None], (SGU_HEADS, CHUNK, CHUNK)),
        "w_up_a": w_up_a[l].astype(BF16),
        "w_up_b": w_up_b[l].astype(BF16),
        "w_out": w_out[l].astype(BF16),
        "ln1_g": ln1_g[l][None, :],
        "ln1_b": ln1_b[l][None, :],
        "wqt": _transpose_cast(peer_wq[l]),
        "kbd": _block_diag_keys(peer_keys[l]).astype(BF16),
        "u_bf": peer_u[l].astype(BF16),
        "vt_bf": _transpose_cast(peer_v[l]),
        "ln2_g": ln2_g[l][None, :],
        "ln2_b": ln2_b[l][None, :],
    }


_TILES = (512, 256, 512, 512, 1024)


def kernel(x_prompt, x_sample, c_prompt, c_sample, w_ada, b_ada, w_in, pool_w, pool_scale, sgu_ln_g,
           sgu_ln_b, sgu_w, sgu_b, w_up_a, w_up_b, w_out, ln1_g, ln1_b, peer_wq, peer_keys, peer_u,
           peer_v, ln2_g, ln2_b):
    assert w_ada.shape[0] == DEPTH == 1
    bp = x_prompt.shape[0]
    bs = x_sample.shape[0]
    c_all = jnp.concatenate([c_prompt, c_sample], axis=0)
    c8 = jnp.pad(c_all, ((0, 8 - bp - bs), (0, 0)))
    mod = _ada(c8, w_ada[0], b_ada[0][None, :]).reshape(8, 6, D_MODEL)
    wts = _prepare(w_ada, b_ada, w_in, pool_w, pool_scale, sgu_ln_g, sgu_ln_b, sgu_w, sgu_b, w_up_a,
                   w_up_b, w_out, ln1_g, ln1_b, peer_wq, peer_keys, peer_u, peer_v, ln2_g, ln2_b)
    y_prompt = _trunk(x_prompt, mod[:bp], wts, _TILES)
    y_sample = _trunk(x_sample, mod[bp:bp + bs], wts, _TILES)
    return (y_prompt, y_sample)
```

```python
import functools
import math

import jax
import jax.numpy as jnp
from jax import lax
from jax.experimental import pallas as pl
from jax.experimental.pallas import tpu as pltpu

F32 = jnp.float32
BF16 = jnp.bfloat16

D_MODEL = 2048
D_A = 1024
D_B = 1024
POOL_HALF = (1, 2, 4, 8)
A_GROUP = 256
CHUNK = 128
SGU_HEADS = 8
PEER_HEADS = 8
N_KEYS = 128
N_EXPERTS = N_KEYS * N_KEYS
TOPK = 16
EXPERT_SUB = 512
DEPTH = 1
ALPHA = (2.0 * DEPTH) ** 0.25
LN_EPS = 1e-5
HALO = 8
LANES = 128
MXU_WIDTH = 256
INPROJ_TILE = 1024
PACK_ROWS = 16
NEG_INF = float("-inf")
VMEM_LIMIT = 60 * 1024 * 1024

_CAND = tuple((k1, k2) for k1 in range(TOPK) for k2 in range(TOPK) if (k1 + 1) * (k2 + 1) <= TOPK)


def _gelu(x):
    c = math.sqrt(2.0 / math.pi)
    return x * (0.5 * (1.0 + jnp.tanh(c * (x + 0.044715 * (x * x * x)))))


def _sigmoid(x):
    return 1.0 / (1.0 + jnp.exp(-x))


def _norm0(x):
    mu = jnp.mean(x, axis=-1, keepdims=True)
    xc = x - mu
    var = jnp.mean(xc * xc, axis=-1, keepdims=True)
    return xc * lax.rsqrt(var + LN_EPS)


def _params(sem, flags=None):
    return pltpu.CompilerParams(dimension_semantics=sem, vmem_limit_bytes=VMEM_LIMIT, flags=flags)


def _ada_kernel(c_ref, w_ref, b_ref, o_ref):
    c = c_ref[...]
    sc = c * _sigmoid(c)
    o_ref[...] = jnp.dot(sc.astype(BF16), w_ref[...].astype(BF16),
                         preferred_element_type=F32) + b_ref[...]


def _ada(c8, w_ada, b_ada):
    n = w_ada.shape[1]
    tn = 1024
    return pl.pallas_call(
        _ada_kernel,
        grid=(n // tn,),
        in_specs=[pl.BlockSpec((8, D_MODEL), lambda j: (0, 0)),
                  pl.BlockSpec((D_MODEL, tn), lambda j: (0, j)),
                  pl.BlockSpec((1, tn), lambda j: (0, j))],
        out_specs=pl.BlockSpec((8, tn), lambda j: (0, j)),
        out_shape=jax.ShapeDtypeStruct((8, n), F32),
        compiler_params=_params(("arbitrary",)),
        name="ada",
    )(c8, w_ada, b_ada)


def _tcast_kernel(x_ref, o_ref):
    o_ref[...] = x_ref[...].T.astype(BF16)


def _transpose_cast(x, tr=512):
    r, c = x.shape
    return pl.pallas_call(
        _tcast_kernel,
        grid=(r // tr,),
        in_specs=[pl.BlockSpec((tr, c), lambda i: (i, 0))],
        out_specs=pl.BlockSpec((c, tr), lambda i: (0, i)),
        out_shape=jax.ShapeDtypeStruct((c, r), BF16),
        compiler_params=_params(("arbitrary",)),
        name="transpose_cast",
    )(x)


def _fold_kernel(a_ref, b_ref, o_ref):
    o_ref[...] = jnp.dot(a_ref[...], b_ref[...], preferred_element_type=F32).astype(BF16)


def _fold(a, b, tn=512):
    m, k = a.shape
    n = b.shape[1]
    return pl.pallas_call(
        _fold_kernel,
        grid=(n // tn,),
        in_specs=[pl.BlockSpec((m, k), lambda j: (0, 0)),
                  pl.BlockSpec((k, tn), lambda j: (0, j))],
        out_specs=pl.BlockSpec((m, tn), lambda j: (0, j)),
        out_shape=jax.ShapeDtypeStruct((m, n), BF16),
        compiler_params=_params(("arbitrary",)),
        name="fold_keys",
    )(a, b)


def _inproj_kernel(x_ref, shift_ref, scale_ref, w_ref, lng_ref, lnb_ref, ws_ref, bs_ref,
                   a_ref, ob_ref, sg_ref):
    tt = x_ref.shape[1]
    d = x_ref.shape[2]
    h = (_norm0(x_ref[0]) * (1.0 + scale_ref[0]) + shift_ref[0]).astype(BF16)
    proj = lambda lo, hi: jnp.dot(h, w_ref[:, lo:hi], preferred_element_type=F32)
    a_ref[0] = proj(0, D_A)
    gu = _gelu(proj(D_A, D_A + D_B)).astype(BF16)
    vn = (_norm0(_gelu(proj(D_A + D_B, D_A + 2 * D_B))) * lng_ref[...] + lnb_ref[...]).astype(BF16)
    for c in range(tt // CHUNK):
        rows = slice(c * CHUNK, (c + 1) * CHUNK)
        for hd in range(SGU_HEADS):
            cols = slice(hd * CHUNK, (hd + 1) * CHUNK)
            mixed = jnp.dot(ws_ref[hd], vn[rows, cols], preferred_element_type=F32) + bs_ref[hd]
            ob_ref[0, rows, cols] = (gu[rows, cols].astype(F32) * mixed).astype(BF16)
    g0 = D_A + 2 * D_B
    for t in range(2 * d // INPROJ_TILE):
        cols = slice(t * INPROJ_TILE, (t + 1) * INPROJ_TILE)
        sg_ref[0, :, cols] = _sigmoid(proj(g0 + t * INPROJ_TILE, g0 + (t + 1) * INPROJ_TILE)).astype(BF16)


def _inproj(x, shift1, scale1, w_in, lng, lnb, ws, bs_full, tt):
    b, s, d = x.shape
    n = w_in.shape[1]
    const = lambda shape: pl.BlockSpec(shape, lambda bi, i: (0,) * len(shape), pipeline_mode=pl.Buffered(1))
    return pl.pallas_call(
        _inproj_kernel,
        grid=(b, s // tt),
        in_specs=[pl.BlockSpec((1, tt, d), lambda bi, i: (bi, i, 0)),
                  pl.BlockSpec((1, 1, d), lambda bi, i: (bi, 0, 0)),
                  pl.BlockSpec((1, 1, d), lambda bi, i: (bi, 0, 0)),
                  const((d, n)),
                  const((1, D_B)),
                  const((1, D_B)),
                  const((SGU_HEADS, CHUNK, CHUNK)),
                  const((SGU_HEADS, CHUNK, CHUNK))],
        out_specs=[pl.BlockSpec((1, tt, D_A), lambda bi, i: (bi, i, 0)),
                   pl.BlockSpec((1, tt, D_B), lambda bi, i: (bi, i, 0)),
                   pl.BlockSpec((1, tt, 2 * d), lambda bi, i: (bi, i, 0))],
        out_shape=[jax.ShapeDtypeStruct((b, s, D_A), F32),
                   jax.ShapeDtypeStruct((b, s, D_B), BF16),
                   jax.ShapeDtypeStruct((b, s, 2 * d), BF16)],
        compiler_params=_params(("arbitrary", "arbitrary")),
        name="inproj",
    )(x, shift1, scale1, w_in, lng, lnb, ws, bs_full)


def _merge_kernel(x_ref, a_ref, ap_ref, an_ref, ob_ref, sg_ref, pw_ref, ps_ref, wua_ref, wub_ref,
                  wo_ref, gate1_ref, g1_ref, b1_ref, shift2_ref, scale2_ref,
                  x1_ref, h2t_ref, ext_scr, *, seq_len):
    i = pl.program_id(1)
    ni = pl.num_programs(1)
    tt = x_ref.shape[1]
    d = x_ref.shape[2]

    ext_scr[0:HALO, :] = jnp.where(i > 0, ap_ref[0], 0.0)
    ext_scr[HALO:HALO + tt, :] = a_ref[0]
    ext_scr[HALO + tt:HALO + tt + HALO, :] = jnp.where(i < ni - 1, an_ref[0], 0.0)

    pos = i * tt + lax.broadcasted_iota(jnp.int32, (tt, A_GROUP), 0)
    ya = jnp.zeros((tt, d), F32)
    for g, hw in enumerate(POOL_HALF):
        cols = slice(g * A_GROUP, (g + 1) * A_GROUP)
        win = ext_scr[HALO - hw:HALO - hw + tt, cols]
        for off in range(-hw + 1, hw):
            win = win + ext_scr[HALO + off:HALO + off + tt, cols]
        cnt = (jnp.minimum(pos + hw, seq_len) - jnp.maximum(pos - hw, 0)).astype(F32)
        pooled = win / cnt - ext_scr[HALO:HALO + tt, cols]
        mixed = jnp.dot(pooled.astype(BF16), pw_ref[g], preferred_element_type=F32) * ps_ref[:, cols]
        ya = ya + jnp.dot(mixed.astype(BF16), wua_ref[cols, :], preferred_element_type=F32)

    yb = jnp.dot(ob_ref[0], wub_ref[...], preferred_element_type=F32)
    merged = sg_ref[0, :, 0:d].astype(F32) * ya + sg_ref[0, :, d:2 * d].astype(F32) * yb
    y = jnp.dot(merged.astype(BF16), wo_ref[...], preferred_element_type=F32)
    x1 = _norm0(ALPHA * x_ref[0] + gate1_ref[0] * y) * g1_ref[...] + b1_ref[...]
    x1_ref[0] = x1
    h2 = _norm0(x1) * (1.0 + scale2_ref[0]) + shift2_ref[0]
    h2t_ref[0] = h2.T.astype(BF16)


def _merge(x, a, ob, sg, pool_w, pool_scale, w_up_a, w_up_b, w_out, gate1, g1, b1, shift2, scale2, tt):
    b, s, d = x.shape
    nh = tt // HALO
    last = s // HALO - 1
    vec = lambda: pl.BlockSpec((1, 1, d), lambda bi, i: (bi, 0, 0))
    row = lambda n: pl.BlockSpec((1, n), lambda bi, i: (0, 0))
    return pl.pallas_call(
        functools.partial(_merge_kernel, seq_len=s),
        grid=(b, s // tt),
        in_specs=[pl.BlockSpec((1, tt, d), lambda bi, i: (bi, i, 0)),
                  pl.BlockSpec((1, tt, D_A), lambda bi, i: (bi, i, 0)),
                  pl.BlockSpec((1, HALO, D_A), lambda bi, i: (bi, jnp.maximum(i * nh - 1, 0), 0)),
                  pl.BlockSpec((1, HALO, D_A), lambda bi, i: (bi, jnp.minimum((i + 1) * nh, last), 0)),
                  pl.BlockSpec((1, tt, D_B), lambda bi, i: (bi, i, 0)),
                  pl.BlockSpec((1, tt, 2 * d), lambda bi, i: (bi, i, 0)),
                  pl.BlockSpec((len(POOL_HALF), A_GROUP, A_GROUP), lambda bi, i: (0, 0, 0)),
                  row(D_A),
                  pl.BlockSpec((D_A, d), lambda bi, i: (0, 0)),
                  pl.BlockSpec((D_B, d), lambda bi, i: (0, 0)),
                  pl.BlockSpec((d, d), lambda bi, i: (0, 0)),
                  vec(), row(d), row(d), vec(), vec()],
        out_specs=[pl.BlockSpec((1, tt, d), lambda bi, i: (bi, i, 0)),
                   pl.BlockSpec((1, d, tt), lambda bi, i: (bi, 0, i))],
        out_shape=[jax.ShapeDtypeStruct((b, s, d), F32),
                   jax.ShapeDtypeStruct((b, d, s), BF16)],
        scratch_shapes=[pltpu.VMEM((tt + 2 * HALO, D_A), F32)],
        compiler_params=_params(("arbitrary", "arbitrary")),
        name="merge",
    )(x, a, a, a, ob, sg, pool_w, pool_scale, w_up_a, w_up_b, w_out, gate1, g1, b1, shift2, scale2)


def _bitonic_merge(xs, descending):
    n = len(xs)
    if n == 1:
        return xs
    h = n // 2
    big = [jnp.maximum(xs[i], xs[i + h]) for i in range(h)]
    small = [jnp.minimum(xs[i], xs[i + h]) for i in range(h)]
    first, second = (big, small) if descending else (small, big)
    return _bitonic_merge(first, descending) + _bitonic_merge(second, descending)


def _bitonic_sort(xs, descending=True):
    n = len(xs)
    if n == 1:
        return xs
    h = n // 2
    return _bitonic_merge(_bitonic_sort(xs[:h], True) + _bitonic_sort(xs[h:], False), descending)


def _merge_top(a, b):
    n = len(a)
    return _bitonic_merge([jnp.maximum(a[i], b[n - 1 - i]) for i in range(n)], True)


def _top_sorted(xs, k):
    groups = [_bitonic_sort(xs[g:g + k]) for g in range(0, len(xs), k)]
    while len(groups) > 1:
        groups = [_merge_top(groups[g], groups[g + 1]) for g in range(0, len(groups), 2)]
    return groups[0]


def _route_kernel(h2t_ref, kq_ref, a_ref, thr_ref, b_ref, s2_scr):
    tt = h2t_ref.shape[2]
    nk, nh = N_KEYS, PEER_HEADS
    half_rows = nk * nh
    neg = jnp.full((nh, LANES), NEG_INF, F32)
    per_piece = MXU_WIDTH // LANES
    piece = lambda p: jnp.dot(kq_ref[...], h2t_ref[0, :, p * MXU_WIDTH:(p + 1) * MXU_WIDTH],
                              preferred_element_type=F32)
    pieces = [piece(0)]

    for c in range(tt // LANES):
        lanes = slice(c * LANES, (c + 1) * LANES)
        if c % per_piece == 1 and len(pieces) < tt // MXU_WIDTH:
            pieces.append(piece(len(pieces)))
        s_all = pieces[c // per_piece][:, (c % per_piece) * LANES:(c % per_piece + 1) * LANES]
        s1 = [s_all[n * nh:(n + 1) * nh, :] for n in range(nk)]
        s2 = [s_all[half_rows + n * nh:half_rows + (n + 1) * nh, :] for n in range(nk)]
        v1 = _top_sorted(s1, TOPK)
        v2 = _top_sorted(s2, TOPK)

        row = lambda k1: [v1[k1] + v2[k2] for k2 in range(TOPK // (k1 + 1))]
        l0 = row(0)
        l1 = row(1) + [neg] * 8
        l2 = _bitonic_sort(row(2) + row(3) + row(4) + [neg] * 4)
        l3 = _bitonic_sort(sum([row(k1) for k1 in range(5, TOPK)], []) + [neg] * 2)
        tau = _merge_top(_merge_top(l0, l1), _merge_top(l2, l3))[TOPK - 1]

        t0 = tau - v2[0]
        e1v = [jnp.exp(v - v1[0]) for v in v1]
        e2v = [jnp.exp(v - v2[0]) for v in v2]
        thrv = [jnp.exp(t0 - v) for v in v1]
        z = jnp.zeros_like(tau)
        for (k1, k2) in _CAND:
            z = z + jnp.where(e2v[k2] >= thrv[k1], e1v[k1] * e2v[k2], 0.0)
        inv_z = 1.0 / z

        for n in range(nk):
            a_ref[0, n, :, lanes] = jnp.where(s1[n] >= v1[TOPK - 1], jnp.exp(s1[n] - v1[0]), 0.0) * inv_z
            thr_ref[0, n, :, lanes] = jnp.exp(t0 - s1[n])

        s2_scr[...] = s_all[half_rows:2 * half_rows, :]
        tau2 = v2[TOPK - 1]
        for hd in range(nh):
            s2h = s2_scr[pl.ds(hd, nk, stride=nh), :]
            b_ref[0, hd, :, lanes] = jnp.where(s2h >= tau2[hd:hd + 1, :],
                                               jnp.exp(s2h - v2[0][hd:hd + 1, :]), 0.0)


def _route(h2t, kq, tt):
    b, d, s = h2t.shape
    nk, nh = N_KEYS, PEER_HEADS
    full = lambda: pl.BlockSpec((d, d), lambda bi, i: (0, 0), pipeline_mode=pl.Buffered(1))
    eh = lambda: pl.BlockSpec((1, nk, nh, tt), lambda bi, i: (bi, 0, 0, i))
    he = lambda: pl.BlockSpec((1, nh, nk, tt), lambda bi, i: (bi, 0, 0, i))
    return pl.pallas_call(
        _route_kernel,
        grid=(b, s // tt),
        in_specs=[pl.BlockSpec((1, d, tt), lambda bi, i: (bi, 0, i)), full()],
        out_specs=[eh(), eh(), he()],
        out_shape=[jax.ShapeDtypeStruct((b, nk, nh, s), F32),
                   jax.ShapeDtypeStruct((b, nk, nh, s), F32),
                   jax.ShapeDtypeStruct((b, nh, nk, s), F32)],
        scratch_shapes=[pltpu.VMEM((nk * nh, LANES), F32)],
        compiler_params=_params(("arbitrary", "arbitrary")),
        name="route",
    )(h2t, kq)


def _experts_step(h2t_ref, u_ref, vt_ref, a_ref, thr_ref, b_ref, acc_scr, xw_scr, hu_next, hu_cur):
    te = u_ref.shape[0]
    tt = h2t_ref.shape[2]
    nsub = te // EXPERT_SUB
    for sub in range(nsub):
        rows = slice(sub * EXPERT_SUB, (sub + 1) * EXPERT_SUB)
        for half in range(tt // MXU_WIDTH):
            cols = slice(half * MXU_WIDTH, (half + 1) * MXU_WIDTH)
            hu_next[rows, cols] = jnp.dot(u_ref[rows, :], h2t_ref[0, :, cols], preferred_element_type=F32)
            for l in range(EXPERT_SUB // N_KEYS):
                e1 = sub * (EXPERT_SUB // N_KEYS) + l
                for c in range(MXU_WIDTH // LANES):
                    lo = half * MXU_WIDTH + c * LANES
                    lanes = slice(lo, lo + LANES)
                    thr = [jnp.broadcast_to(thr_ref[0, e1, hd:hd + 1, lanes], (PACK_ROWS, LANES))
                           for hd in range(PEER_HEADS)]
                    a = [jnp.broadcast_to(a_ref[0, e1, hd:hd + 1, lanes], (PACK_ROWS, LANES))
                         for hd in range(PEER_HEADS)]
                    for r in range(N_KEYS // PACK_ROWS):
                        k2 = slice(r * PACK_ROWS, (r + 1) * PACK_ROWS)
                        w = None
                        for hd in range(PEER_HEADS):
                            bv = b_ref[0, hd, k2, lanes]
                            term = jnp.where(bv >= thr[hd], bv, 0.0) * a[hd]
                            w = term if w is None else w + term
                        k2s = slice(e1 * N_KEYS + r * PACK_ROWS, e1 * N_KEYS + (r + 1) * PACK_ROWS)
                        xw_scr[k2s, lanes] = (w * _gelu(hu_cur[k2s, lanes])).astype(BF16)
            acc_scr[:, cols] += jnp.dot(vt_ref[:, rows], xw_scr[rows, cols], preferred_element_type=F32)


def _experts_kernel(h2t_ref, u_ref, vt_ref, a_ref, thr_ref, b_ref, x1_ref, gate2_ref,
                    g2_ref, b2_ref, y_ref, acc_scr, hu_even, hu_odd, xw_scr, *, ne):
    s = pl.program_id(1)

    @pl.when(s == 0)
    def _():
        acc_scr[...] = jnp.zeros_like(acc_scr)
        hu_odd[...] = jnp.zeros_like(hu_odd)

    step = functools.partial(_experts_step, h2t_ref, u_ref, vt_ref, a_ref, thr_ref, b_ref,
                             acc_scr, xw_scr)

    @pl.when(s % 2 == 0)
    def _():
        step(hu_even, hu_odd)

    @pl.when(s % 2 == 1)
    def _():
        step(hu_odd, hu_even)

    @pl.when((s >= 1) & ((s - 1) % ne == ne - 1))
    def _():
        y2 = acc_scr[...].T
        y_ref[0] = _norm0(ALPHA * x1_ref[0] + gate2_ref[0] * y2) * g2_ref[...] + b2_ref[...]
        acc_scr[...] = jnp.zeros_like(acc_scr)


def _experts(h2t, u_bf, vt_bf, a, thr, bm, x1, gate2, g2, b2, tt, te):
    b, d, s = h2t.shape
    nk, nh = N_KEYS, PEER_HEADS
    nl = te // nk
    ni = s // tt
    ne = N_EXPERTS // te
    nxt_i = lambda st: jnp.minimum(st // ne, ni - 1)
    cur = lambda st: jnp.maximum(st - 1, 0)
    eh = lambda: pl.BlockSpec((1, nl, nh, tt), lambda bi, st: (bi, cur(st) % ne, 0, cur(st) // ne))
    he = lambda: pl.BlockSpec((1, nh, nk, tt), lambda bi, st: (bi, 0, 0, cur(st) // ne))
    row = lambda: pl.BlockSpec((1, d), lambda bi, st: (0, 0))
    return pl.pallas_call(
        functools.partial(_experts_kernel, ne=ne),
        grid=(b, ni * ne + 1),
        in_specs=[pl.BlockSpec((1, d, tt), lambda bi, st: (bi, 0, nxt_i(st))),
                  pl.BlockSpec((te, d), lambda bi, st: (st % ne, 0)),
                  pl.BlockSpec((d, te), lambda bi, st: (0, cur(st) % ne)),
                  eh(), eh(), he(),
                  pl.BlockSpec((1, tt, d), lambda bi, st: (bi, cur(st) // ne, 0)),
                  pl.BlockSpec((1, 1, d), lambda bi, st: (bi, 0, 0)),
                  row(), row()],
        out_specs=pl.BlockSpec((1, tt, d), lambda bi, st: (bi, cur(st) // ne, 0)),
        out_shape=jax.ShapeDtypeStruct((b, s, d), F32),
        scratch_shapes=[pltpu.VMEM((d, tt), F32),
                        pltpu.VMEM((te, tt), F32),
                        pltpu.VMEM((te, tt), F32),
                        pltpu.VMEM((te, tt), BF16)],
        compiler_params=_params(("arbitrary", "arbitrary")),
        name="experts",
    )(h2t, u_bf, vt_bf, a, thr, bm, x1, gate2, g2, b2)


def _block_diag_keys(keys):
    nh, _, nk, c = keys.shape
    eye = jnp.eye(nh, dtype=keys.dtype)
    full = jnp.einsum("hknc,hg,kj->knhgjc", keys, eye, jnp.eye(2, dtype=keys.dtype))
    return full.reshape(2 * nk * nh, nh * 2 * c)


def _trunk(x, mod, wts, tiles):
    b, s, d = x.shape
    shift1, scale1, gate1, shift2, scale2, gate2 = [mod[:, k][:, None, :] for k in range(6)]
    t_in, t_merge, t_route, t_exp, t_e = tiles
    a, ob, sg = _inproj(x, shift1, scale1, wts["w_in"], wts["sgu_ln_g"], wts["sgu_ln_b"],
                        wts["sgu_w"], wts["sgu_b_full"], min(t_in, s))
    x1, h2t = _merge(x, a, ob, sg, wts["pool_w"], wts["pool_scale"], wts["w_up_a"], wts["w_up_b"],
                     wts["w_out"], gate1, wts["ln1_g"], wts["ln1_b"], shift2, scale2, min(t_merge, s))
    ra, rthr, rb = _route(h2t, wts["kq"], min(t_route, s))
    return _experts(h2t, wts["u_bf"], wts["vt_bf"], ra, rthr, rb, x1, gate2,
                    wts["ln2_g"], wts["ln2_b"], min(t_exp, s), t_e)


def _prepare(w_ada, b_ada, w_in, pool_w, pool_scale, sgu_ln_g, sgu_ln_b, sgu_w, sgu_b, w_up_a, w_up_b,
             w_out, ln1_g, ln1_b, peer_wq, peer_keys, peer_u, peer_v, ln2_g, ln2_b):
    l = 0
    return {
        "w_in": w_in[l].astype(BF16),
        "pool_w": pool_w[l].astype(BF16),
        "pool_scale": pool_scale[l][None, :],
        "sgu_ln_g": sgu_ln_g[l][None, :],
        "sgu_ln_b": sgu_ln_b[l][None, :],
        "sgu_w": sgu_w[l].astype(BF16),
        "sgu_b_full": jnp.broadcast_to(sgu_b[l][:, :, None], (SGU_HEADS, CHUNK, CHUNK)),
        "w_up_a": w_up_a[l].astype(BF16),
        "w_up_b": w_up_b[l].astype(BF16),
        "w_out": w_out[l].astype(BF16),
        "ln1_g": ln1_g[l][None, :],
        "ln1_b": ln1_b[l][None, :],
        "kq": _fold(_block_diag_keys(peer_keys[l]).astype(BF16), _transpose_cast(peer_wq[l])),
        "u_bf": peer_u[l].astype(BF16),
        "vt_bf": _transpose_cast(peer_v[l]),
        "ln2_g": ln2_g[l][None, :],
        "ln2_b": ln2_b[l][None, :],
    }


_TILES = (256, 256, 512, 512, 1024)


def kernel(x_prompt, x_sample, c_prompt, c_sample, w_ada, b_ada, w_in, pool_w, pool_scale, sgu_ln_g,
           sgu_ln_b, sgu_w, sgu_b, w_up_a, w_up_b, w_out, ln1_g, ln1_b, peer_wq, peer_keys, peer_u,
           peer_v, ln2_g, ln2_b):
    assert w_ada.shape[0] == DEPTH == 1
    bp = x_prompt.shape[0]
    bs = x_sample.shape[0]
    c_all = jnp.concatenate([c_prompt, c_sample], axis=0)
    c8 = jnp.pad(c_all, ((0, 8 - bp - bs), (0, 0)))
    mod = _ada(c8, w_ada[0], b_ada[0][None, :]).reshape(8, 6, D_MODEL)
    wts = _prepare(w_ada, b_ada, w_in, pool_w, pool_scale, sgu_ln_g, sgu_ln_b, sgu_w, sgu_b, w_up_a,
                   w_up_b, w_out, ln1_g, ln1_b, peer_wq, peer_keys, peer_u, peer_v, ln2_g, ln2_b)
    y_prompt = _trunk(x_prompt, mod[:bp], wts, _TILES)
    y_sample = _trunk(x_sample, mod[bp:bp + bs], wts, _TILES)
    return (y_prompt, y_sample)
```

```python
import functools
import math

import jax
import jax.numpy as jnp
from jax import lax
from jax.experimental import pallas as pl
from jax.experimental.pallas import tpu as pltpu

F32 = jnp.float32
BF16 = jnp.bfloat16

D_MODEL = 2048
D_A = 1024
D_B = 1024
POOL_HALF = (1, 2, 4, 8)
A_GROUP = 256
CHUNK = 128
SGU_HEADS = 8
PEER_HEADS = 8
N_KEYS = 128
N_EXPERTS = N_KEYS * N_KEYS
TOPK = 16
EXPERT_SUB = 512
DEPTH = 1
ALPHA = (2.0 * DEPTH) ** 0.25
LN_EPS = 1e-5
HALO = 8
LANES = 128
MXU_WIDTH = 256
INPROJ_TILE = 1024
PACK_ROWS = 16
NEG_INF = float("-inf")
VMEM_LIMIT = 60 * 1024 * 1024

_CAND = tuple((k1, k2) for k1 in range(TOPK) for k2 in range(TOPK) if (k1 + 1) * (k2 + 1) <= TOPK)


def _gelu(x):
    c = math.sqrt(2.0 / math.pi)
    return x * (0.5 * (1.0 + jnp.tanh(c * (x + 0.044715 * (x * x * x)))))


def _sigmoid(x):
    return 1.0 / (1.0 + jnp.exp(-x))


def _norm0(x):
    mu = jnp.mean(x, axis=-1, keepdims=True)
    xc = x - mu
    var = jnp.mean(xc * xc, axis=-1, keepdims=True)
    return xc * lax.rsqrt(var + LN_EPS)


def _params(sem, flags=None):
    return pltpu.CompilerParams(dimension_semantics=sem, vmem_limit_bytes=VMEM_LIMIT, flags=flags)


def _ada_kernel(c_ref, w_ref, b_ref, o_ref):
    c = c_ref[...]
    sc = c * _sigmoid(c)
    o_ref[...] = jnp.dot(sc.astype(BF16), w_ref[...].astype(BF16),
                         preferred_element_type=F32) + b_ref[...]


def _ada(c8, w_ada, b_ada):
    n = w_ada.shape[1]
    tn = 1024
    return pl.pallas_call(
        _ada_kernel,
        grid=(n // tn,),
        in_specs=[pl.BlockSpec((8, D_MODEL), lambda j: (0, 0)),
                  pl.BlockSpec((D_MODEL, tn), lambda j: (0, j)),
                  pl.BlockSpec((1, tn), lambda j: (0, j))],
        out_specs=pl.BlockSpec((8, tn), lambda j: (0, j)),
        out_shape=jax.ShapeDtypeStruct((8, n), F32),
        compiler_params=_params(("arbitrary",)),
        name="ada",
    )(c8, w_ada, b_ada)


def _tcast_kernel(x_ref, o_ref):
    o_ref[...] = x_ref[...].T.astype(BF16)


def _transpose_cast(x, tr=512):
    r, c = x.shape
    return pl.pallas_call(
        _tcast_kernel,
        grid=(r // tr,),
        in_specs=[pl.BlockSpec((tr, c), lambda i: (i, 0))],
        out_specs=pl.BlockSpec((c, tr), lambda i: (0, i)),
        out_shape=jax.ShapeDtypeStruct((c, r), BF16),
        compiler_params=_params(("arbitrary",)),
        name="transpose_cast",
    )(x)


def _tcast_tile_kernel(x_ref, o_ref):
    o_ref[0] = x_ref[...].T.astype(BF16)


def _transpose_cast_tiles(x, tr):
    r, c = x.shape
    return pl.pallas_call(
        _tcast_tile_kernel,
        grid=(r // tr,),
        in_specs=[pl.BlockSpec((tr, c), lambda i: (i, 0))],
        out_specs=pl.BlockSpec((1, c, tr), lambda i: (i, 0, 0)),
        out_shape=jax.ShapeDtypeStruct((r // tr, c, tr), BF16),
        compiler_params=_params(("arbitrary",)),
        name="transpose_cast_tiles",
    )(x)


def _fold_kernel(a_ref, b_ref, o_ref):
    o_ref[...] = jnp.dot(a_ref[...], b_ref[...], preferred_element_type=F32).astype(BF16)


def _fold(a, b, tn=512):
    m, k = a.shape
    n = b.shape[1]
    return pl.pallas_call(
        _fold_kernel,
        grid=(n // tn,),
        in_specs=[pl.BlockSpec((m, k), lambda j: (0, 0)),
                  pl.BlockSpec((k, tn), lambda j: (0, j))],
        out_specs=pl.BlockSpec((m, tn), lambda j: (0, j)),
        out_shape=jax.ShapeDtypeStruct((m, n), BF16),
        compiler_params=_params(("arbitrary",)),
        name="fold_keys",
    )(a, b)


def _inproj_kernel(x_ref, shift_ref, scale_ref, w_ref, lng_ref, lnb_ref, ws_ref, bs_ref,
                   a_ref, ob_ref, sg_ref):
    tt = x_ref.shape[1]
    d = x_ref.shape[2]
    h = (_norm0(x_ref[0]) * (1.0 + scale_ref[0]) + shift_ref[0]).astype(BF16)
    proj = lambda lo, hi: jnp.dot(h, w_ref[:, lo:hi], preferred_element_type=F32)
    a_ref[0] = proj(0, D_A)
    gu = _gelu(proj(D_A, D_A + D_B)).astype(BF16)
    vn = (_norm0(_gelu(proj(D_A + D_B, D_A + 2 * D_B))) * lng_ref[...] + lnb_ref[...]).astype(BF16)
    for c in range(tt // CHUNK):
        rows = slice(c * CHUNK, (c + 1) * CHUNK)
        for hd in range(SGU_HEADS):
            cols = slice(hd * CHUNK, (hd + 1) * CHUNK)
            mixed = jnp.dot(ws_ref[hd], vn[rows, cols], preferred_element_type=F32) + bs_ref[hd]
            ob_ref[0, rows, cols] = (gu[rows, cols].astype(F32) * mixed).astype(BF16)
    g0 = D_A + 2 * D_B
    for t in range(2 * d // INPROJ_TILE):
        cols = slice(t * INPROJ_TILE, (t + 1) * INPROJ_TILE)
        sg_ref[0, :, cols] = _sigmoid(proj(g0 + t * INPROJ_TILE, g0 + (t + 1) * INPROJ_TILE)).astype(BF16)


def _inproj(x, shift1, scale1, w_in, lng, lnb, ws, bs_full, tt):
    b, s, d = x.shape
    n = w_in.shape[1]
    const = lambda shape: pl.BlockSpec(shape, lambda bi, i: (0,) * len(shape), pipeline_mode=pl.Buffered(1))
    return pl.pallas_call(
        _inproj_kernel,
        grid=(b, s // tt),
        in_specs=[pl.BlockSpec((1, tt, d), lambda bi, i: (bi, i, 0)),
                  pl.BlockSpec((1, 1, d), lambda bi, i: (bi, 0, 0)),
                  pl.BlockSpec((1, 1, d), lambda bi, i: (bi, 0, 0)),
                  const((d, n)),
                  const((1, D_B)),
                  const((1, D_B)),
                  const((SGU_HEADS, CHUNK, CHUNK)),
                  const((SGU_HEADS, CHUNK, CHUNK))],
        out_specs=[pl.BlockSpec((1, tt, D_A), lambda bi, i: (bi, i, 0)),
                   pl.BlockSpec((1, tt, D_B), lambda bi, i: (bi, i, 0)),
                   pl.BlockSpec((1, tt, 2 * d), lambda bi, i: (bi, i, 0))],
        out_shape=[jax.ShapeDtypeStruct((b, s, D_A), F32),
                   jax.ShapeDtypeStruct((b, s, D_B), BF16),
                   jax.ShapeDtypeStruct((b, s, 2 * d), BF16)],
        compiler_params=_params(("arbitrary", "arbitrary")),
        name="inproj",
    )(x, shift1, scale1, w_in, lng, lnb, ws, bs_full)


def _merge_kernel(x_ref, a_ref, ap_ref, an_ref, ob_ref, sg_ref, pw_ref, ps_ref, wua_ref, wub_ref,
                  wo_ref, gate1_ref, g1_ref, b1_ref, shift2_ref, scale2_ref,
                  x1_ref, h2t_ref, ext_scr, *, seq_len):
    i = pl.program_id(1)
    ni = pl.num_programs(1)
    tt = x_ref.shape[1]
    d = x_ref.shape[2]

    ext_scr[0:HALO, :] = jnp.where(i > 0, ap_ref[0], 0.0)
    ext_scr[HALO:HALO + tt, :] = a_ref[0]
    ext_scr[HALO + tt:HALO + tt + HALO, :] = jnp.where(i < ni - 1, an_ref[0], 0.0)

    pos = i * tt + lax.broadcasted_iota(jnp.int32, (tt, A_GROUP), 0)
    ya = jnp.zeros((tt, d), F32)
    for g, hw in enumerate(POOL_HALF):
        cols = slice(g * A_GROUP, (g + 1) * A_GROUP)
        win = ext_scr[HALO - hw:HALO - hw + tt, cols]
        for off in range(-hw + 1, hw):
            win = win + ext_scr[HALO + off:HALO + off + tt, cols]
        cnt = (jnp.minimum(pos + hw, seq_len) - jnp.maximum(pos - hw, 0)).astype(F32)
        pooled = win / cnt - ext_scr[HALO:HALO + tt, cols]
        mixed = jnp.dot(pooled.astype(BF16), pw_ref[g], preferred_element_type=F32) * ps_ref[:, cols]
        ya = ya + jnp.dot(mixed.astype(BF16), wua_ref[cols, :], preferred_element_type=F32)

    yb = jnp.dot(ob_ref[0], wub_ref[...], preferred_element_type=F32)
    merged = sg_ref[0, :, 0:d].astype(F32) * ya + sg_ref[0, :, d:2 * d].astype(F32) * yb
    y = jnp.dot(merged.astype(BF16), wo_ref[...], preferred_element_type=F32)
    x1 = _norm0(ALPHA * x_ref[0] + gate1_ref[0] * y) * g1_ref[...] + b1_ref[...]
    x1_ref[0] = x1
    h2 = _norm0(x1) * (1.0 + scale2_ref[0]) + shift2_ref[0]
    h2t_ref[0] = h2.T.astype(BF16)


def _merge(x, a, ob, sg, pool_w, pool_scale, w_up_a, w_up_b, w_out, gate1, g1, b1, shift2, scale2, tt):
    b, s, d = x.shape
    nh = tt // HALO
    last = s // HALO - 1
    vec = lambda: pl.BlockSpec((1, 1, d), lambda bi, i: (bi, 0, 0))
    row = lambda n: pl.BlockSpec((1, n), lambda bi, i: (0, 0))
    return pl.pallas_call(
        functools.partial(_merge_kernel, seq_len=s),
        grid=(b, s // tt),
        in_specs=[pl.BlockSpec((1, tt, d), lambda bi, i: (bi, i, 0)),
                  pl.BlockSpec((1, tt, D_A), lambda bi, i: (bi, i, 0)),
                  pl.BlockSpec((1, HALO, D_A), lambda bi, i: (bi, jnp.maximum(i * nh - 1, 0), 0)),
                  pl.BlockSpec((1, HALO, D_A), lambda bi, i: (bi, jnp.minimum((i + 1) * nh, last), 0)),
                  pl.BlockSpec((1, tt, D_B), lambda bi, i: (bi, i, 0)),
                  pl.BlockSpec((1, tt, 2 * d), lambda bi, i: (bi, i, 0)),
                  pl.BlockSpec((len(POOL_HALF), A_GROUP, A_GROUP), lambda bi, i: (0, 0, 0)),
                  row(D_A),
                  pl.BlockSpec((D_A, d), lambda bi, i: (0, 0)),
                  pl.BlockSpec((D_B, d), lambda bi, i: (0, 0)),
                  pl.BlockSpec((d, d), lambda bi, i: (0, 0)),
                  vec(), row(d), row(d), vec(), vec()],
        out_specs=[pl.BlockSpec((1, tt, d), lambda bi, i: (bi, i, 0)),
                   pl.BlockSpec((1, d, tt), lambda bi, i: (bi, 0, i))],
        out_shape=[jax.ShapeDtypeStruct((b, s, d), F32),
                   jax.ShapeDtypeStruct((b, d, s), BF16)],
        scratch_shapes=[pltpu.VMEM((tt + 2 * HALO, D_A), F32)],
        compiler_params=_params(("arbitrary", "arbitrary")),
        name="merge",
    )(x, a, a, a, ob, sg, pool_w, pool_scale, w_up_a, w_up_b, w_out, gate1, g1, b1, shift2, scale2)


def _bitonic_merge(xs, descending):
    n = len(xs)
    if n == 1:
        return xs
    h = n // 2
    big = [jnp.maximum(xs[i], xs[i + h]) for i in range(h)]
    small = [jnp.minimum(xs[i], xs[i + h]) for i in range(h)]
    first, second = (big, small) if descending else (small, big)
    return _bitonic_merge(first, descending) + _bitonic_merge(second, descending)


def _bitonic_sort(xs, descending=True):
    n = len(xs)
    if n == 1:
        return xs
    h = n // 2
    return _bitonic_merge(_bitonic_sort(xs[:h], True) + _bitonic_sort(xs[h:], False), descending)


def _merge_top(a, b):
    n = len(a)
    return _bitonic_merge([jnp.maximum(a[i], b[n - 1 - i]) for i in range(n)], True)


def _top_sorted(xs, k):
    groups = [_bitonic_sort(xs[g:g + k]) for g in range(0, len(xs), k)]
    while len(groups) > 1:
        groups = [_merge_top(groups[g], groups[g + 1]) for g in range(0, len(groups), 2)]
    return groups[0]


def _route_kernel(h2t_ref, kq_ref, a_ref, thr_ref, b_ref, s2_scr):
    tt = h2t_ref.shape[2]
    nk, nh = N_KEYS, PEER_HEADS
    half_rows = nk * nh
    neg = jnp.full((nh, LANES), NEG_INF, F32)
    per_piece = MXU_WIDTH // LANES
    piece = lambda p: jnp.dot(kq_ref[...], h2t_ref[0, :, p * MXU_WIDTH:(p + 1) * MXU_WIDTH],
                              preferred_element_type=F32)
    pieces = [piece(0)]

    for c in range(tt // LANES):
        lanes = slice(c * LANES, (c + 1) * LANES)
        if c % per_piece == 1 and len(pieces) < tt // MXU_WIDTH:
            pieces.append(piece(len(pieces)))
        s_all = pieces[c // per_piece][:, (c % per_piece) * LANES:(c % per_piece + 1) * LANES]
        s1 = [s_all[n * nh:(n + 1) * nh, :] for n in range(nk)]
        s2 = [s_all[half_rows + n * nh:half_rows + (n + 1) * nh, :] for n in range(nk)]
        v1 = _top_sorted(s1, TOPK)
        v2 = _top_sorted(s2, TOPK)

        row = lambda k1: [v1[k1] + v2[k2] for k2 in range(TOPK // (k1 + 1))]
        l0 = row(0)
        l1 = row(1) + [neg] * 8
        l2 = _bitonic_sort(row(2) + row(3) + row(4) + [neg] * 4)
        l3 = _bitonic_sort(sum([row(k1) for k1 in range(5, TOPK)], []) + [neg] * 2)
        tau = _merge_top(_merge_top(l0, l1), _merge_top(l2, l3))[TOPK - 1]

        t0 = tau - v2[0]
        e1v = [jnp.exp(v - v1[0]) for v in v1]
        e2v = [jnp.exp(v - v2[0]) for v in v2]
        thrv = [jnp.exp(t0 - v) for v in v1]
        z = jnp.zeros_like(tau)
        for (k1, k2) in _CAND:
            z = z + jnp.where(e2v[k2] >= thrv[k1], e1v[k1] * e2v[k2], 0.0)
        inv_z = 1.0 / z

        for n in range(nk):
            a_ref[0, n, :, lanes] = jnp.where(s1[n] >= v1[TOPK - 1], jnp.exp(s1[n] - v1[0]), 0.0) * inv_z
            thr_ref[0, n, :, lanes] = jnp.exp(t0 - s1[n])

        s2_scr[...] = s_all[half_rows:2 * half_rows, :]
        tau2 = v2[TOPK - 1]
        for hd in range(nh):
            s2h = s2_scr[pl.ds(hd, nk, stride=nh), :]
            b_ref[0, hd, :, lanes] = jnp.where(s2h >= tau2[hd:hd + 1, :],
                                               jnp.exp(s2h - v2[0][hd:hd + 1, :]), 0.0)


def _route(h2t, kq, tt):
    b, d, s = h2t.shape
    nk, nh = N_KEYS, PEER_HEADS
    full = lambda: pl.BlockSpec((d, d), lambda bi, i: (0, 0), pipeline_mode=pl.Buffered(1))
    eh = lambda: pl.BlockSpec((1, nk, nh, tt), lambda bi, i: (bi, 0, 0, i))
    he = lambda: pl.BlockSpec((1, nh, nk, tt), lambda bi, i: (bi, 0, 0, i))
    return pl.pallas_call(
        _route_kernel,
        grid=(b, s // tt),
        in_specs=[pl.BlockSpec((1, d, tt), lambda bi, i: (bi, 0, i)), full()],
        out_specs=[eh(), eh(), he()],
        out_shape=[jax.ShapeDtypeStruct((b, nk, nh, s), F32),
                   jax.ShapeDtypeStruct((b, nk, nh, s), F32),
                   jax.ShapeDtypeStruct((b, nh, nk, s), F32)],
        scratch_shapes=[pltpu.VMEM((nk * nh, LANES), F32)],
        compiler_params=_params(("arbitrary", "arbitrary")),
        name="route",
    )(h2t, kq)


def _experts_step(h2t_ref, u_ref, vt_ref, a_ref, thr_ref, b_ref, acc_scr, xw_scr, hu_next, hu_cur):
    te = u_ref.shape[0]
    tt = h2t_ref.shape[2]
    nsub = te // EXPERT_SUB
    for sub in range(nsub):
        rows = slice(sub * EXPERT_SUB, (sub + 1) * EXPERT_SUB)
        for half in range(tt // MXU_WIDTH):
            cols = slice(half * MXU_WIDTH, (half + 1) * MXU_WIDTH)
            hu_next[rows, cols] = jnp.dot(u_ref[rows, :], h2t_ref[0, :, cols], preferred_element_type=F32)
            for l in range(EXPERT_SUB // N_KEYS):
                e1 = sub * (EXPERT_SUB // N_KEYS) + l
                for c in range(MXU_WIDTH // LANES):
                    lo = half * MXU_WIDTH + c * LANES
                    lanes = slice(lo, lo + LANES)
                    thr = [jnp.broadcast_to(thr_ref[0, e1, hd:hd + 1, lanes], (PACK_ROWS, LANES))
                           for hd in range(PEER_HEADS)]
                    a = [jnp.broadcast_to(a_ref[0, e1, hd:hd + 1, lanes], (PACK_ROWS, LANES))
                         for hd in range(PEER_HEADS)]
                    for r in range(N_KEYS // PACK_ROWS):
                        k2 = slice(r * PACK_ROWS, (r + 1) * PACK_ROWS)
                        w = None
                        for hd in range(PEER_HEADS):
                            bv = b_ref[0, hd, k2, lanes]
                            term = jnp.where(bv >= thr[hd], bv, 0.0) * a[hd]
                            w = term if w is None else w + term
                        k2s = slice(e1 * N_KEYS + r * PACK_ROWS, e1 * N_KEYS + (r + 1) * PACK_ROWS)
                        xw_scr[k2s, lanes] = (w * _gelu(hu_cur[k2s, lanes])).astype(BF16)
            acc_scr[:, cols] += jnp.dot(vt_ref[0, :, rows], xw_scr[rows, cols], preferred_element_type=F32)


def _experts_kernel(h2t_ref, u_ref, vt_ref, a_ref, thr_ref, b_ref, x1_ref, gate2_ref,
                    g2_ref, b2_ref, y_ref, acc_scr, hu_even, hu_odd, xw_scr, *, ne):
    s = pl.program_id(1)

    @pl.when(s == 0)
    def _():
        acc_scr[...] = jnp.zeros_like(acc_scr)
        hu_odd[...] = jnp.zeros_like(hu_odd)

    step = functools.partial(_experts_step, h2t_ref, u_ref, vt_ref, a_ref, thr_ref, b_ref,
                             acc_scr, xw_scr)

    @pl.when(s % 2 == 0)
    def _():
        step(hu_even, hu_odd)

    @pl.when(s % 2 == 1)
    def _():
        step(hu_odd, hu_even)

    @pl.when((s >= 1) & ((s - 1) % ne == ne - 1))
    def _():
        y2 = acc_scr[...].T
        y_ref[0] = _norm0(ALPHA * x1_ref[0] + gate2_ref[0] * y2) * g2_ref[...] + b2_ref[...]
        acc_scr[...] = jnp.zeros_like(acc_scr)


def _experts(h2t, u_bf, vt_bf, a, thr, bm, x1, gate2, g2, b2, tt, te):
    b, d, s = h2t.shape
    nk, nh = N_KEYS, PEER_HEADS
    nl = te // nk
    ni = s // tt
    ne = N_EXPERTS // te
    nxt_i = lambda st: jnp.minimum(st // ne, ni - 1)
    cur = lambda st: jnp.maximum(st - 1, 0)
    eh = lambda: pl.BlockSpec((1, nl, nh, tt), lambda bi, st: (bi, cur(st) % ne, 0, cur(st) // ne))
    he = lambda: pl.BlockSpec((1, nh, nk, tt), lambda bi, st: (bi, 0, 0, cur(st) // ne))
    row = lambda: pl.BlockSpec((1, d), lambda bi, st: (0, 0))
    return pl.pallas_call(
        functools.partial(_experts_kernel, ne=ne),
        grid=(b, ni * ne + 1),
        in_specs=[pl.BlockSpec((1, d, tt), lambda bi, st: (bi, 0, nxt_i(st))),
                  pl.BlockSpec((te, d), lambda bi, st: (st % ne, 0)),
                  pl.BlockSpec((1, d, te), lambda bi, st: (cur(st) % ne, 0, 0)),
                  eh(), eh(), he(),
                  pl.BlockSpec((1, tt, d), lambda bi, st: (bi, cur(st) // ne, 0)),
                  pl.BlockSpec((1, 1, d), lambda bi, st: (bi, 0, 0)),
                  row(), row()],
        out_specs=pl.BlockSpec((1, tt, d), lambda bi, st: (bi, cur(st) // ne, 0)),
        out_shape=jax.ShapeDtypeStruct((b, s, d), F32),
        scratch_shapes=[pltpu.VMEM((d, tt), F32),
                        pltpu.VMEM((te, tt), F32),
                        pltpu.VMEM((te, tt), F32),
                        pltpu.VMEM((te, tt), BF16)],
        compiler_params=_params(("arbitrary", "arbitrary")),
        name="experts",
    )(h2t, u_bf, vt_bf, a, thr, bm, x1, gate2, g2, b2)


def _block_diag_keys(keys):
    nh, _, nk, c = keys.shape
    eye = jnp.eye(nh, dtype=keys.dtype)
    full = jnp.einsum("hknc,hg,kj->knhgjc", keys, eye, jnp.eye(2, dtype=keys.dtype))
    return full.reshape(2 * nk * nh, nh * 2 * c)


def _trunk(x, mod, wts, tiles):
    b, s, d = x.shape
    shift1, scale1, gate1, shift2, scale2, gate2 = [mod[:, k][:, None, :] for k in range(6)]
    t_in, t_merge, t_route, t_exp, t_e = tiles
    a, ob, sg = _inproj(x, shift1, scale1, wts["w_in"], wts["sgu_ln_g"], wts["sgu_ln_b"],
                        wts["sgu_w"], wts["sgu_b_full"], min(t_in, s))
    x1, h2t = _merge(x, a, ob, sg, wts["pool_w"], wts["pool_scale"], wts["w_up_a"], wts["w_up_b"],
                     wts["w_out"], gate1, wts["ln1_g"], wts["ln1_b"], shift2, scale2, min(t_merge, s))
    ra, rthr, rb = _route(h2t, wts["kq"], min(t_route, s))
    return _experts(h2t, wts["u_bf"], wts["vt_bf"], ra, rthr, rb, x1, gate2,
                    wts["ln2_g"], wts["ln2_b"], min(t_exp, s), t_e)


def _prepare(w_ada, b_ada, w_in, pool_w, pool_scale, sgu_ln_g, sgu_ln_b, sgu_w, sgu_b, w_up_a, w_up_b,
             w_out, ln1_g, ln1_b, peer_wq, peer_keys, peer_u, peer_v, ln2_g, ln2_b):
    l = 0
    return {
        "w_in": w_in[l].astype(BF16),
        "pool_w": pool_w[l].astype(BF16),
        "pool_scale": pool_scale[l][None, :],
        "sgu_ln_g": sgu_ln_g[l][None, :],
        "sgu_ln_b": sgu_ln_b[l][None, :],
        "sgu_w": sgu_w[l].astype(BF16),
        "sgu_b_full": jnp.broadcast_to(sgu_b[l][:, :, None], (SGU_HEADS, CHUNK, CHUNK)),
        "w_up_a": w_up_a[l].astype(BF16),
        "w_up_b": w_up_b[l].astype(BF16),
        "w_out": w_out[l].astype(BF16),
        "ln1_g": ln1_g[l][None, :],
        "ln1_b": ln1_b[l][None, :],
        "kq": _fold(_block_diag_keys(peer_keys[l]).astype(BF16), _transpose_cast(peer_wq[l])),
        "u_bf": peer_u[l].astype(BF16),
        "vt_bf": _transpose_cast_tiles(peer_v[l], _TILES[4]),
        "ln2_g": ln2_g[l][None, :],
        "ln2_b": ln2_b[l][None, :],
    }


_TILES = (256, 256, 512, 512, 1024)


def kernel(x_prompt, x_sample, c_prompt, c_sample, w_ada, b_ada, w_in, pool_w, pool_scale, sgu_ln_g,
           sgu_ln_b, sgu_w, sgu_b, w_up_a, w_up_b, w_out, ln1_g, ln1_b, peer_wq, peer_keys, peer_u,
           peer_v, ln2_g, ln2_b):
    assert w_ada.shape[0] == DEPTH == 1
    bp = x_prompt.shape[0]
    bs = x_sample.shape[0]
    c_all = jnp.concatenate([c_prompt, c_sample], axis=0)
    c8 = jnp.pad(c_all, ((0, 8 - bp - bs), (0, 0)))
    mod = _ada(c8, w_ada[0], b_ada[0][None, :]).reshape(8, 6, D_MODEL)
    wts = _prepare(w_ada, b_ada, w_in, pool_w, pool_scale, sgu_ln_g, sgu_ln_b, sgu_w, sgu_b, w_up_a,
                   w_up_b, w_out, ln1_g, ln1_b, peer_wq, peer_keys, peer_u, peer_v, ln2_g, ln2_b)
    y_prompt = _trunk(x_prompt, mod[:bp], wts, _TILES)
    y_sample = _trunk(x_sample, mod[bp:bp + bs], wts, _TILES)
    return (y_prompt, y_sample)
```

```python
import functools
import math

import jax
import jax.numpy as jnp
from jax import lax
from jax.experimental import pallas as pl
from jax.experimental.pallas import tpu as pltpu

F32 = jnp.float32
BF16 = jnp.bfloat16

D_MODEL = 2048
D_A = 1024
D_B = 1024
POOL_HALF = (1, 2, 4, 8)
A_GROUP = 256
CHUNK = 128
SGU_HEADS = 8
PEER_HEADS = 8
N_KEYS = 128
N_EXPERTS = N_KEYS * N_KEYS
TOPK = 16
EXPERT_SUB = 512
DEPTH = 1
ALPHA = (2.0 * DEPTH) ** 0.25
LN_EPS = 1e-5
HALO = 8
LANES = 128
MXU_WIDTH = 256
INPROJ_TILE = 1024
PACK_ROWS = 16
NEG_INF = float("-inf")
VMEM_LIMIT = 60 * 1024 * 1024

_CAND = tuple((k1, k2) for k1 in range(TOPK) for k2 in range(TOPK) if (k1 + 1) * (k2 + 1) <= TOPK)


def _gelu(x):
    c = math.sqrt(2.0 / math.pi)
    return x * (0.5 * (1.0 + jnp.tanh(c * (x + 0.044715 * (x * x * x)))))


def _sigmoid(x):
    return 1.0 / (1.0 + jnp.exp(-x))


def _norm0(x):
    mu = jnp.mean(x, axis=-1, keepdims=True)
    xc = x - mu
    var = jnp.mean(xc * xc, axis=-1, keepdims=True)
    return xc * lax.rsqrt(var + LN_EPS)


def _params(sem, flags=None):
    return pltpu.CompilerParams(dimension_semantics=sem, vmem_limit_bytes=VMEM_LIMIT, flags=flags)


def _ada_kernel(c_ref, w_ref, b_ref, o_ref):
    c = c_ref[...]
    sc = c * _sigmoid(c)
    o_ref[...] = jnp.dot(sc.astype(BF16), w_ref[...].astype(BF16),
                         preferred_element_type=F32) + b_ref[...]


def _ada(c8, w_ada, b_ada):
    n = w_ada.shape[1]
    tn = 1024
    return pl.pallas_call(
        _ada_kernel,
        grid=(n // tn,),
        in_specs=[pl.BlockSpec((8, D_MODEL), lambda j: (0, 0)),
                  pl.BlockSpec((D_MODEL, tn), lambda j: (0, j)),
                  pl.BlockSpec((1, tn), lambda j: (0, j))],
        out_specs=pl.BlockSpec((8, tn), lambda j: (0, j)),
        out_shape=jax.ShapeDtypeStruct((8, n), F32),
        compiler_params=_params(("arbitrary",)),
        name="ada",
    )(c8, w_ada, b_ada)


def _tcast_kernel(x_ref, o_ref):
    o_ref[...] = x_ref[...].T.astype(BF16)


def _transpose_cast(x, tr=512):
    r, c = x.shape
    return pl.pallas_call(
        _tcast_kernel,
        grid=(r // tr,),
        in_specs=[pl.BlockSpec((tr, c), lambda i: (i, 0))],
        out_specs=pl.BlockSpec((c, tr), lambda i: (0, i)),
        out_shape=jax.ShapeDtypeStruct((c, r), BF16),
        compiler_params=_params(("arbitrary",)),
        name="transpose_cast",
    )(x)


def _fold_kernel(a_ref, b_ref, o_ref):
    o_ref[...] = jnp.dot(a_ref[...], b_ref[...], preferred_element_type=F32).astype(BF16)


def _fold(a, b, tn=512):
    m, k = a.shape
    n = b.shape[1]
    return pl.pallas_call(
        _fold_kernel,
        grid=(n // tn,),
        in_specs=[pl.BlockSpec((m, k), lambda j: (0, 0)),
                  pl.BlockSpec((k, tn), lambda j: (0, j))],
        out_specs=pl.BlockSpec((m, tn), lambda j: (0, j)),
        out_shape=jax.ShapeDtypeStruct((m, n), BF16),
        compiler_params=_params(("arbitrary",)),
        name="fold_keys",
    )(a, b)


def _inproj_kernel(x_ref, shift_ref, scale_ref, w_ref, lng_ref, lnb_ref, ws_ref, bs_ref,
                   a_ref, ob_ref, sg_ref):
    tt = x_ref.shape[1]
    d = x_ref.shape[2]
    h = (_norm0(x_ref[0]) * (1.0 + scale_ref[0]) + shift_ref[0]).astype(BF16)
    proj = lambda lo, hi: jnp.dot(h, w_ref[:, lo:hi], preferred_element_type=F32)
    a_ref[0] = proj(0, D_A)
    gu = _gelu(proj(D_A, D_A + D_B)).astype(BF16)
    vn = (_norm0(_gelu(proj(D_A + D_B, D_A + 2 * D_B))) * lng_ref[...] + lnb_ref[...]).astype(BF16)
    for c in range(tt // CHUNK):
        rows = slice(c * CHUNK, (c + 1) * CHUNK)
        for hd in range(SGU_HEADS):
            cols = slice(hd * CHUNK, (hd + 1) * CHUNK)
            mixed = jnp.dot(ws_ref[hd], vn[rows, cols], preferred_element_type=F32) + bs_ref[hd]
            ob_ref[0, rows, cols] = (gu[rows, cols].astype(F32) * mixed).astype(BF16)
    g0 = D_A + 2 * D_B
    for t in range(2 * d // INPROJ_TILE):
        cols = slice(t * INPROJ_TILE, (t + 1) * INPROJ_TILE)
        sg_ref[0, :, cols] = _sigmoid(proj(g0 + t * INPROJ_TILE, g0 + (t + 1) * INPROJ_TILE)).astype(BF16)


def _inproj(x, shift1, scale1, w_in, lng, lnb, ws, bs_full, tt):
    b, s, d = x.shape
    n = w_in.shape[1]
    const = lambda shape: pl.BlockSpec(shape, lambda bi, i: (0,) * len(shape), pipeline_mode=pl.Buffered(1))
    return pl.pallas_call(
        _inproj_kernel,
        grid=(b, s // tt),
        in_specs=[pl.BlockSpec((1, tt, d), lambda bi, i: (bi, i, 0)),
                  pl.BlockSpec((1, 1, d), lambda bi, i: (bi, 0, 0)),
                  pl.BlockSpec((1, 1, d), lambda bi, i: (bi, 0, 0)),
                  const((d, n)),
                  const((1, D_B)),
                  const((1, D_B)),
                  const((SGU_HEADS, CHUNK, CHUNK)),
                  const((SGU_HEADS, CHUNK, CHUNK))],
        out_specs=[pl.BlockSpec((1, tt, D_A), lambda bi, i: (bi, i, 0)),
                   pl.BlockSpec((1, tt, D_B), lambda bi, i: (bi, i, 0)),
                   pl.BlockSpec((1, tt, 2 * d), lambda bi, i: (bi, i, 0))],
        out_shape=[jax.ShapeDtypeStruct((b, s, D_A), F32),
                   jax.ShapeDtypeStruct((b, s, D_B), BF16),
                   jax.ShapeDtypeStruct((b, s, 2 * d), BF16)],
        compiler_params=_params(("arbitrary", "arbitrary")),
        name="inproj",
    )(x, shift1, scale1, w_in, lng, lnb, ws, bs_full)


def _merge_kernel(x_ref, a_ref, ap_ref, an_ref, ob_ref, sg_ref, pw_ref, ps_ref, wua_ref, wub_ref,
                  wo_ref, gate1_ref, g1_ref, b1_ref, shift2_ref, scale2_ref,
                  x1_ref, h2t_ref, ext_scr, *, seq_len):
    i = pl.program_id(1)
    ni = pl.num_programs(1)
    tt = x_ref.shape[1]
    d = x_ref.shape[2]

    ext_scr[0:HALO, :] = jnp.where(i > 0, ap_ref[0], 0.0)
    ext_scr[HALO:HALO + tt, :] = a_ref[0]
    ext_scr[HALO + tt:HALO + tt + HALO, :] = jnp.where(i < ni - 1, an_ref[0], 0.0)

    pos = i * tt + lax.broadcasted_iota(jnp.int32, (tt, A_GROUP), 0)
    ya = jnp.zeros((tt, d), F32)
    for g, hw in enumerate(POOL_HALF):
        cols = slice(g * A_GROUP, (g + 1) * A_GROUP)
        win = ext_scr[HALO - hw:HALO - hw + tt, cols]
        for off in range(-hw + 1, hw):
            win = win + ext_scr[HALO + off:HALO + off + tt, cols]
        cnt = (jnp.minimum(pos + hw, seq_len) - jnp.maximum(pos - hw, 0)).astype(F32)
        pooled = win / cnt - ext_scr[HALO:HALO + tt, cols]
        mixed = jnp.dot(pooled.astype(BF16), pw_ref[g], preferred_element_type=F32) * ps_ref[:, cols]
        ya = ya + jnp.dot(mixed.astype(BF16), wua_ref[cols, :], preferred_element_type=F32)

    yb = jnp.dot(ob_ref[0], wub_ref[...], preferred_element_type=F32)
    merged = sg_ref[0, :, 0:d].astype(F32) * ya + sg_ref[0, :, d:2 * d].astype(F32) * yb
    y = jnp.dot(merged.astype(BF16), wo_ref[...], preferred_element_type=F32)
    x1 = _norm0(ALPHA * x_ref[0] + gate1_ref[0] * y) * g1_ref[...] + b1_ref[...]
    x1_ref[0] = x1
    h2 = _norm0(x1) * (1.0 + scale2_ref[0]) + shift2_ref[0]
    h2t_ref[0] = h2.T.astype(BF16)


def _merge(x, a, ob, sg, pool_w, pool_scale, w_up_a, w_up_b, w_out, gate1, g1, b1, shift2, scale2, tt):
    b, s, d = x.shape
    nh = tt // HALO
    last = s // HALO - 1
    vec = lambda: pl.BlockSpec((1, 1, d), lambda bi, i: (bi, 0, 0))
    row = lambda n: pl.BlockSpec((1, n), lambda bi, i: (0, 0))
    return pl.pallas_call(
        functools.partial(_merge_kernel, seq_len=s),
        grid=(b, s // tt),
        in_specs=[pl.BlockSpec((1, tt, d), lambda bi, i: (bi, i, 0)),
                  pl.BlockSpec((1, tt, D_A), lambda bi, i: (bi, i, 0)),
                  pl.BlockSpec((1, HALO, D_A), lambda bi, i: (bi, jnp.maximum(i * nh - 1, 0), 0)),
                  pl.BlockSpec((1, HALO, D_A), lambda bi, i: (bi, jnp.minimum((i + 1) * nh, last), 0)),
                  pl.BlockSpec((1, tt, D_B), lambda bi, i: (bi, i, 0)),
                  pl.BlockSpec((1, tt, 2 * d), lambda bi, i: (bi, i, 0)),
                  pl.BlockSpec((len(POOL_HALF), A_GROUP, A_GROUP), lambda bi, i: (0, 0, 0)),
                  row(D_A),
                  pl.BlockSpec((D_A, d), lambda bi, i: (0, 0)),
                  pl.BlockSpec((D_B, d), lambda bi, i: (0, 0)),
                  pl.BlockSpec((d, d), lambda bi, i: (0, 0)),
                  vec(), row(d), row(d), vec(), vec()],
        out_specs=[pl.BlockSpec((1, tt, d), lambda bi, i: (bi, i, 0)),
                   pl.BlockSpec((1, d, tt), lambda bi, i: (bi, 0, i))],
        out_shape=[jax.ShapeDtypeStruct((b, s, d), F32),
                   jax.ShapeDtypeStruct((b, d, s), BF16)],
        scratch_shapes=[pltpu.VMEM((tt + 2 * HALO, D_A), F32)],
        compiler_params=_params(("arbitrary", "arbitrary")),
        name="merge",
    )(x, a, a, a, ob, sg, pool_w, pool_scale, w_up_a, w_up_b, w_out, gate1, g1, b1, shift2, scale2)


def _bitonic_merge(xs, descending):
    n = len(xs)
    if n == 1:
        return xs
    h = n // 2
    big = [jnp.maximum(xs[i], xs[i + h]) for i in range(h)]
    small = [jnp.minimum(xs[i], xs[i + h]) for i in range(h)]
    first, second = (big, small) if descending else (small, big)
    return _bitonic_merge(first, descending) + _bitonic_merge(second, descending)


def _bitonic_sort(xs, descending=True):
    n = len(xs)
    if n == 1:
        return xs
    h = n // 2
    return _bitonic_merge(_bitonic_sort(xs[:h], True) + _bitonic_sort(xs[h:], False), descending)


def _merge_top(a, b):
    n = len(a)
    return _bitonic_merge([jnp.maximum(a[i], b[n - 1 - i]) for i in range(n)], True)


def _top_sorted(xs, k):
    groups = [_bitonic_sort(xs[g:g + k]) for g in range(0, len(xs), k)]
    while len(groups) > 1:
        groups = [_merge_top(groups[g], groups[g + 1]) for g in range(0, len(groups), 2)]
    return groups[0]


def _route_kernel(h2t_ref, kq_ref, a_ref, thr_ref, b_ref, s2_scr):
    tt = h2t_ref.shape[2]
    nk, nh = N_KEYS, PEER_HEADS
    half_rows = nk * nh
    neg = jnp.full((nh, LANES), NEG_INF, F32)
    per_piece = MXU_WIDTH // LANES
    piece = lambda p: jnp.dot(kq_ref[...], h2t_ref[0, :, p * MXU_WIDTH:(p + 1) * MXU_WIDTH],
                              preferred_element_type=F32)
    pieces = [piece(0)]

    for c in range(tt // LANES):
        lanes = slice(c * LANES, (c + 1) * LANES)
        if c % per_piece == 1 and len(pieces) < tt // MXU_WIDTH:
            pieces.append(piece(len(pieces)))
        s_all = pieces[c // per_piece][:, (c % per_piece) * LANES:(c % per_piece + 1) * LANES]
        s1 = [s_all[n * nh:(n + 1) * nh, :] for n in range(nk)]
        s2 = [s_all[half_rows + n * nh:half_rows + (n + 1) * nh, :] for n in range(nk)]
        v1 = _top_sorted(s1, TOPK)
        v2 = _top_sorted(s2, TOPK)

        row = lambda k1: [v1[k1] + v2[k2] for k2 in range(TOPK // (k1 + 1))]
        l0 = row(0)
        l1 = row(1) + [neg] * 8
        l2 = _bitonic_sort(row(2) + row(3) + row(4) + [neg] * 4)
        l3 = _bitonic_sort(sum([row(k1) for k1 in range(5, TOPK)], []) + [neg] * 2)
        tau = _merge_top(_merge_top(l0, l1), _merge_top(l2, l3))[TOPK - 1]

        t0 = tau - v2[0]
        e1v = [jnp.exp(v - v1[0]) for v in v1]
        e2v = [jnp.exp(v - v2[0]) for v in v2]
        thrv = [jnp.exp(t0 - v) for v in v1]
        z = jnp.zeros_like(tau)
        for (k1, k2) in _CAND:
            z = z + jnp.where(e2v[k2] >= thrv[k1], e1v[k1] * e2v[k2], 0.0)
        inv_z = 1.0 / z

        for n in range(nk):
            a_ref[0, n, :, lanes] = jnp.where(s1[n] >= v1[TOPK - 1], jnp.exp(s1[n] - v1[0]), 0.0) * inv_z
            thr_ref[0, n, :, lanes] = jnp.exp(t0 - s1[n])

        s2_scr[...] = s_all[half_rows:2 * half_rows, :]
        tau2 = v2[TOPK - 1]
        for hd in range(nh):
            s2h = s2_scr[pl.ds(hd, nk, stride=nh), :]
            b_ref[0, hd, :, lanes] = jnp.where(s2h >= tau2[hd:hd + 1, :],
                                               jnp.exp(s2h - v2[0][hd:hd + 1, :]), 0.0)


def _route(h2t, kq, tt):
    b, d, s = h2t.shape
    nk, nh = N_KEYS, PEER_HEADS
    full = lambda: pl.BlockSpec((d, d), lambda bi, i: (0, 0), pipeline_mode=pl.Buffered(1))
    eh = lambda: pl.BlockSpec((1, nk, nh, tt), lambda bi, i: (bi, 0, 0, i))
    he = lambda: pl.BlockSpec((1, nh, nk, tt), lambda bi, i: (bi, 0, 0, i))
    return pl.pallas_call(
        _route_kernel,
        grid=(b, s // tt),
        in_specs=[pl.BlockSpec((1, d, tt), lambda bi, i: (bi, 0, i)), full()],
        out_specs=[eh(), eh(), he()],
        out_shape=[jax.ShapeDtypeStruct((b, nk, nh, s), F32),
                   jax.ShapeDtypeStruct((b, nk, nh, s), F32),
                   jax.ShapeDtypeStruct((b, nh, nk, s), F32)],
        scratch_shapes=[pltpu.VMEM((nk * nh, LANES), F32)],
        compiler_params=_params(("arbitrary", "arbitrary")),
        name="route",
    )(h2t, kq)


def _experts_step(h2t_ref, u_ref, vt_ref, a_ref, thr_ref, b_ref, acc_scr, xw_scr, hu_next, hu_cur):
    te = u_ref.shape[0]
    tt = h2t_ref.shape[2]
    nsub = te // EXPERT_SUB
    for sub in range(nsub):
        rows = slice(sub * EXPERT_SUB, (sub + 1) * EXPERT_SUB)
        for half in range(tt // MXU_WIDTH):
            cols = slice(half * MXU_WIDTH, (half + 1) * MXU_WIDTH)
            hu_next[rows, cols] = jnp.dot(u_ref[rows, :], h2t_ref[0, :, cols], preferred_element_type=F32)
            for l in range(EXPERT_SUB // N_KEYS):
                e1 = sub * (EXPERT_SUB // N_KEYS) + l
                for c in range(MXU_WIDTH // LANES):
                    lo = half * MXU_WIDTH + c * LANES
                    lanes = slice(lo, lo + LANES)
                    thr = [jnp.broadcast_to(thr_ref[0, e1, hd:hd + 1, lanes], (PACK_ROWS, LANES))
                           for hd in range(PEER_HEADS)]
                    a = [jnp.broadcast_to(a_ref[0, e1, hd:hd + 1, lanes], (PACK_ROWS, LANES))
                         for hd in range(PEER_HEADS)]
                    for r in range(N_KEYS // PACK_ROWS):
                        k2 = slice(r * PACK_ROWS, (r + 1) * PACK_ROWS)
                        w = None
                        for hd in range(PEER_HEADS):
                            bv = b_ref[0, hd, k2, lanes]
                            term = jnp.where(bv >= thr[hd], bv, 0.0) * a[hd]
                            w = term if w is None else w + term
                        k2s = slice(e1 * N_KEYS + r * PACK_ROWS, e1 * N_KEYS + (r + 1) * PACK_ROWS)
                        xw_scr[k2s, lanes] = (w * _gelu(hu_cur[k2s, lanes])).astype(BF16)
            acc_scr[:, cols] += jnp.dot(vt_ref[:, rows], xw_scr[rows, cols], preferred_element_type=F32)


def _experts_kernel(h2t_ref, u_ref, vt_ref, a_ref, thr_ref, b_ref, x1_ref, gate2_ref,
                    g2_ref, b2_ref, y_ref, acc_scr, hu_even, hu_odd, xw_scr, *, ne):
    s = pl.program_id(1)

    @pl.when(s == 0)
    def _():
        acc_scr[...] = jnp.zeros_like(acc_scr)
        hu_odd[...] = jnp.zeros_like(hu_odd)

    _experts_step(h2t_ref, u_ref, vt_ref, a_ref, thr_ref, b_ref, acc_scr, xw_scr, hu_even, hu_odd)

    @pl.when(s >= 0)
    def _():
        hu_odd[...] = hu_even[...]

    @pl.when((s >= 1) & ((s - 1) % ne == ne - 1))
    def _():
        y2 = acc_scr[...].T
        y_ref[0] = _norm0(ALPHA * x1_ref[0] + gate2_ref[0] * y2) * g2_ref[...] + b2_ref[...]
        acc_scr[...] = jnp.zeros_like(acc_scr)


def _experts(h2t, u_bf, vt_bf, a, thr, bm, x1, gate2, g2, b2, tt, te):
    b, d, s = h2t.shape
    nk, nh = N_KEYS, PEER_HEADS
    nl = te // nk
    ni = s // tt
    ne = N_EXPERTS // te
    nxt_i = lambda st: jnp.minimum(st // ne, ni - 1)
    cur = lambda st: jnp.maximum(st - 1, 0)
    eh = lambda: pl.BlockSpec((1, nl, nh, tt), lambda bi, st: (bi, cur(st) % ne, 0, cur(st) // ne))
    he = lambda: pl.BlockSpec((1, nh, nk, tt), lambda bi, st: (bi, 0, 0, cur(st) // ne))
    row = lambda: pl.BlockSpec((1, d), lambda bi, st: (0, 0))
    return pl.pallas_call(
        functools.partial(_experts_kernel, ne=ne),
        grid=(b, ni * ne + 1),
        in_specs=[pl.BlockSpec((1, d, tt), lambda bi, st: (bi, 0, nxt_i(st))),
                  pl.BlockSpec((te, d), lambda bi, st: (st % ne, 0)),
                  pl.BlockSpec((d, te), lambda bi, st: (0, cur(st) % ne)),
                  eh(), eh(), he(),
                  pl.BlockSpec((1, tt, d), lambda bi, st: (bi, cur(st) // ne, 0)),
                  pl.BlockSpec((1, 1, d), lambda bi, st: (bi, 0, 0)),
                  row(), row()],
        out_specs=pl.BlockSpec((1, tt, d), lambda bi, st: (bi, cur(st) // ne, 0)),
        out_shape=jax.ShapeDtypeStruct((b, s, d), F32),
        scratch_shapes=[pltpu.VMEM((d, tt), F32),
                        pltpu.VMEM((te, tt), F32),
                        pltpu.VMEM((te, tt), F32),
                        pltpu.VMEM((te, tt), BF16)],
        compiler_params=_params(("arbitrary", "arbitrary")),
        name="experts",
    )(h2t, u_bf, vt_bf, a, thr, bm, x1, gate2, g2, b2)


def _block_diag_keys(keys):
    nh, _, nk, c = keys.shape
    eye = jnp.eye(nh, dtype=keys.dtype)
    full = jnp.einsum("hknc,hg,kj->knhgjc", keys, eye, jnp.eye(2, dtype=keys.dtype))
    return full.reshape(2 * nk * nh, nh * 2 * c)


def _trunk(x, mod, wts, tiles):
    b, s, d = x.shape
    shift1, scale1, gate1, shift2, scale2, gate2 = [mod[:, k][:, None, :] for k in range(6)]
    t_in, t_merge, t_route, t_exp, t_e = tiles
    a, ob, sg = _inproj(x, shift1, scale1, wts["w_in"], wts["sgu_ln_g"], wts["sgu_ln_b"],
                        wts["sgu_w"], wts["sgu_b_full"], min(t_in, s))
    x1, h2t = _merge(x, a, ob, sg, wts["pool_w"], wts["pool_scale"], wts["w_up_a"], wts["w_up_b"],
                     wts["w_out"], gate1, wts["ln1_g"], wts["ln1_b"], shift2, scale2, min(t_merge, s))
    ra, rthr, rb = _route(h2t, wts["kq"], min(t_route, s))
    return _experts(h2t, wts["u_bf"], wts["vt_bf"], ra, rthr, rb, x1, gate2,
                    wts["ln2_g"], wts["ln2_b"], min(t_exp, s), t_e)


def _prepare(w_ada, b_ada, w_in, pool_w, pool_scale, sgu_ln_g, sgu_ln_b, sgu_w, sgu_b, w_up_a, w_up_b,
             w_out, ln1_g, ln1_b, peer_wq, peer_keys, peer_u, peer_v, ln2_g, ln2_b):
    l = 0
    return {
        "w_in": w_in[l].astype(BF16),
        "pool_w": pool_w[l].astype(BF16),
        "pool_scale": pool_scale[l][None, :],
        "sgu_ln_g": sgu_ln_g[l][None, :],
        "sgu_ln_b": sgu_ln_b[l][None, :],
        "sgu_w": sgu_w[l].astype(BF16),
        "sgu_b_full": jnp.broadcast_to(sgu_b[l][:, :, None], (SGU_HEADS, CHUNK, CHUNK)),
        "w_up_a": w_up_a[l].astype(BF16),
        "w_up_b": w_up_b[l].astype(BF16),
        "w_out": w_out[l].astype(BF16),
        "ln1_g": ln1_g[l][None, :],
        "ln1_b": ln1_b[l][None, :],
        "kq": _fold(_block_diag_keys(peer_keys[l]).astype(BF16), _transpose_cast(peer_wq[l])),
        "u_bf": peer_u[l].astype(BF16),
        "vt_bf": _transpose_cast(peer_v[l]),
        "ln2_g": ln2_g[l][None, :],
        "ln2_b": ln2_b[l][None, :],
    }


_TILES = (256, 256, 512, 512, 1024)


def kernel(x_prompt, x_sample, c_prompt, c_sample, w_ada, b_ada, w_in, pool_w, pool_scale, sgu_ln_g,
           sgu_ln_b, sgu_w, sgu_b, w_up_a, w_up_b, w_out, ln1_g, ln1_b, peer_wq, peer_keys, peer_u,
           peer_v, ln2_g, ln2_b):
    assert w_ada.shape[0] == DEPTH == 1
    bp = x_prompt.shape[0]
    bs = x_sample.shape[0]
    c_all = jnp.concatenate([c_prompt, c_sample], axis=0)
    c8 = jnp.pad(c_all, ((0, 8 - bp - bs), (0, 0)))
    mod = _ada(c8, w_ada[0], b_ada[0][None, :]).reshape(8, 6, D_MODEL)
    wts = _prepare(w_ada, b_ada, w_in, pool_w, pool_scale, sgu_ln_g, sgu_ln_b, sgu_w, sgu_b, w_up_a,
                   w_up_b, w_out, ln1_g, ln1_b, peer_wq, peer_keys, peer_u, peer_v, ln2_g, ln2_b)
    y_prompt = _trunk(x_prompt, mod[:bp], wts, _TILES)
    y_sample = _trunk(x_sample, mod[bp:bp + bs], wts, _TILES)
    return (y_prompt, y_sample)
```

```python
import functools
import math

import jax
import jax.numpy as jnp
from jax import lax
from jax.experimental import pallas as pl
from jax.experimental.pallas import tpu as pltpu

F32 = jnp.float32
BF16 = jnp.bfloat16

D_MODEL = 2048
D_A = 1024
D_B = 1024
POOL_HALF = (1, 2, 4, 8)
A_GROUP = 256
CHUNK = 128
SGU_HEADS = 8
PEER_HEADS = 8
N_KEYS = 128
N_EXPERTS = N_KEYS * N_KEYS
TOPK = 16
EXPERT_SUB = 256
DEPTH = 1
ALPHA = (2.0 * DEPTH) ** 0.25
LN_EPS = 1e-5
HALO = 8
LANES = 128
MXU_WIDTH = 256
EXPERT_PIECE = MXU_WIDTH
INPROJ_TILE = 1024
PREP_TILE = 512
ADA_TILE = 1024
PACK_ROWS = 16
NEG_INF = float("-inf")
VMEM_LIMIT = 60 * 1024 * 1024

_CAND = tuple((k1, k2) for k1 in range(TOPK) for k2 in range(TOPK) if (k1 + 1) * (k2 + 1) <= TOPK)


def _gelu(x):
    c = math.sqrt(2.0 / math.pi)
    return x * (0.5 * (1.0 + jnp.tanh(c * (x + 0.044715 * (x * x * x)))))


def _sigmoid(x):
    return 1.0 / (1.0 + jnp.exp(-x))


def _norm0(x):
    mu = jnp.mean(x, axis=-1, keepdims=True)
    xc = x - mu
    var = jnp.mean(xc * xc, axis=-1, keepdims=True)
    return xc * lax.rsqrt(var + LN_EPS)


def _params(sem):
    return pltpu.CompilerParams(dimension_semantics=sem, vmem_limit_bytes=VMEM_LIMIT)


def _ada_kernel(c_ref, w_ref, b_ref, o_ref):
    c = c_ref[...]
    sc = c * _sigmoid(c)
    o_ref[...] = jnp.dot(sc.astype(BF16), w_ref[...].astype(BF16),
                         preferred_element_type=F32) + b_ref[...]


def _ada(c8, w_ada, b_ada):
    n = w_ada.shape[1]
    tn = ADA_TILE
    return pl.pallas_call(
        _ada_kernel,
        grid=(n // tn,),
        in_specs=[pl.BlockSpec((8, D_MODEL), lambda j: (0, 0)),
                  pl.BlockSpec((D_MODEL, tn), lambda j: (0, j)),
                  pl.BlockSpec((1, tn), lambda j: (0, j))],
        out_specs=pl.BlockSpec((8, tn), lambda j: (0, j)),
        out_shape=jax.ShapeDtypeStruct((8, n), F32),
        compiler_params=_params(("arbitrary",)),
        name="ada",
    )(c8, w_ada, b_ada)


def _tcast_kernel(x_ref, o_ref):
    o_ref[...] = x_ref[...].T.astype(BF16)


def _transpose_cast(x):
    r, c = x.shape
    tr = PREP_TILE
    return pl.pallas_call(
        _tcast_kernel,
        grid=(r // tr,),
        in_specs=[pl.BlockSpec((tr, c), lambda i: (i, 0))],
        out_specs=pl.BlockSpec((c, tr), lambda i: (0, i)),
        out_shape=jax.ShapeDtypeStruct((c, r), BF16),
        compiler_params=_params(("arbitrary",)),
        name="transpose_cast",
    )(x)


def _fold_kernel(a_ref, b_ref, o_ref):
    o_ref[...] = jnp.dot(a_ref[...], b_ref[...], preferred_element_type=F32).astype(BF16)


def _fold(a, b):
    m, k = a.shape
    n = b.shape[1]
    tn = PREP_TILE
    return pl.pallas_call(
        _fold_kernel,
        grid=(n // tn,),
        in_specs=[pl.BlockSpec((m, k), lambda j: (0, 0)),
                  pl.BlockSpec((k, tn), lambda j: (0, j))],
        out_specs=pl.BlockSpec((m, tn), lambda j: (0, j)),
        out_shape=jax.ShapeDtypeStruct((m, n), BF16),
        compiler_params=_params(("arbitrary",)),
        name="fold_keys",
    )(a, b)


def _inproj_kernel(x_ref, shift_ref, scale_ref, w_ref, lng_ref, lnb_ref, ws_ref, bs_ref,
                   a_ref, ob_ref, sg_ref):
    tt = x_ref.shape[1]
    d = x_ref.shape[2]
    h = (_norm0(x_ref[0]) * (1.0 + scale_ref[0]) + shift_ref[0]).astype(BF16)
    proj = lambda lo, hi: jnp.dot(h, w_ref[:, lo:hi], preferred_element_type=F32)
    a_ref[0] = proj(0, D_A)
    gu = _gelu(proj(D_A, D_A + D_B)).astype(BF16)
    vn = (_norm0(_gelu(proj(D_A + D_B, D_A + 2 * D_B))) * lng_ref[...] + lnb_ref[...]).astype(BF16)
    for c in range(tt // CHUNK):
        rows = slice(c * CHUNK, (c + 1) * CHUNK)
        for hd in range(SGU_HEADS):
            cols = slice(hd * CHUNK, (hd + 1) * CHUNK)
            mixed = jnp.dot(ws_ref[hd], vn[rows, cols], preferred_element_type=F32) + bs_ref[hd]
            ob_ref[0, rows, cols] = (gu[rows, cols].astype(F32) * mixed).astype(BF16)
    g0 = D_A + 2 * D_B
    for t in range(2 * d // INPROJ_TILE):
        cols = slice(t * INPROJ_TILE, (t + 1) * INPROJ_TILE)
        sg_ref[0, :, cols] = _sigmoid(proj(g0 + t * INPROJ_TILE, g0 + (t + 1) * INPROJ_TILE)).astype(BF16)


def _inproj(x, shift1, scale1, w_in, lng, lnb, ws, bs_full, tt):
    b, s, d = x.shape
    n = w_in.shape[1]
    const = lambda shape: pl.BlockSpec(shape, lambda bi, i: (0,) * len(shape), pipeline_mode=pl.Buffered(1))
    return pl.pallas_call(
        _inproj_kernel,
        grid=(b, s // tt),
        in_specs=[pl.BlockSpec((1, tt, d), lambda bi, i: (bi, i, 0)),
                  pl.BlockSpec((1, 1, d), lambda bi, i: (bi, 0, 0)),
                  pl.BlockSpec((1, 1, d), lambda bi, i: (bi, 0, 0)),
                  const((d, n)),
                  const((1, D_B)),
                  const((1, D_B)),
                  const((SGU_HEADS, CHUNK, CHUNK)),
                  const((SGU_HEADS, CHUNK, CHUNK))],
        out_specs=[pl.BlockSpec((1, tt, D_A), lambda bi, i: (bi, i, 0)),
                   pl.BlockSpec((1, tt, D_B), lambda bi, i: (bi, i, 0)),
                   pl.BlockSpec((1, tt, 2 * d), lambda bi, i: (bi, i, 0))],
        out_shape=[jax.ShapeDtypeStruct((b, s, D_A), F32),
                   jax.ShapeDtypeStruct((b, s, D_B), BF16),
                   jax.ShapeDtypeStruct((b, s, 2 * d), BF16)],
        compiler_params=_params(("arbitrary", "arbitrary")),
        name="inproj",
    )(x, shift1, scale1, w_in, lng, lnb, ws, bs_full)


def _merge_kernel(x_ref, a_ref, ap_ref, an_ref, ob_ref, sg_ref, pw_ref, ps_ref, wua_ref, wub_ref,
                  wo_ref, gate1_ref, g1_ref, b1_ref, shift2_ref, scale2_ref,
                  x1_ref, h2t_ref, ext_scr, *, seq_len):
    i = pl.program_id(1)
    ni = pl.num_programs(1)
    tt = x_ref.shape[1]
    d = x_ref.shape[2]

    ext_scr[0:HALO, :] = jnp.where(i > 0, ap_ref[0], 0.0)
    ext_scr[HALO:HALO + tt, :] = a_ref[0]
    ext_scr[HALO + tt:HALO + tt + HALO, :] = jnp.where(i < ni - 1, an_ref[0], 0.0)

    pos = i * tt + lax.broadcasted_iota(jnp.int32, (tt, A_GROUP), 0)
    ya = jnp.zeros((tt, d), F32)
    for g, hw in enumerate(POOL_HALF):
        cols = slice(g * A_GROUP, (g + 1) * A_GROUP)
        win = ext_scr[HALO - hw:HALO - hw + tt, cols]
        for off in range(-hw + 1, hw):
            win = win + ext_scr[HALO + off:HALO + off + tt, cols]
        cnt = (jnp.minimum(pos + hw, seq_len) - jnp.maximum(pos - hw, 0)).astype(F32)
        pooled = win / cnt - ext_scr[HALO:HALO + tt, cols]
        mixed = jnp.dot(pooled.astype(BF16), pw_ref[g], preferred_element_type=F32) * ps_ref[:, cols]
        ya = ya + jnp.dot(mixed.astype(BF16), wua_ref[cols, :], preferred_element_type=F32)

    yb = jnp.dot(ob_ref[0], wub_ref[...], preferred_element_type=F32)
    merged = sg_ref[0, :, 0:d].astype(F32) * ya + sg_ref[0, :, d:2 * d].astype(F32) * yb
    y = jnp.dot(merged.astype(BF16), wo_ref[...], preferred_element_type=F32)
    x1 = _norm0(ALPHA * x_ref[0] + gate1_ref[0] * y) * g1_ref[...] + b1_ref[...]
    x1_ref[0] = x1
    h2 = _norm0(x1) * (1.0 + scale2_ref[0]) + shift2_ref[0]
    h2t_ref[0] = h2.T.astype(BF16)


def _merge(x, a, ob, sg, pool_w, pool_scale, w_up_a, w_up_b, w_out, gate1, g1, b1, shift2, scale2, tt):
    b, s, d = x.shape
    nh = tt // HALO
    last = s // HALO - 1
    vec = lambda: pl.BlockSpec((1, 1, d), lambda bi, i: (bi, 0, 0))
    row = lambda n: pl.BlockSpec((1, n), lambda bi, i: (0, 0))
    return pl.pallas_call(
        functools.partial(_merge_kernel, seq_len=s),
        grid=(b, s // tt),
        in_specs=[pl.BlockSpec((1, tt, d), lambda bi, i: (bi, i, 0)),
                  pl.BlockSpec((1, tt, D_A), lambda bi, i: (bi, i, 0)),
                  pl.BlockSpec((1, HALO, D_A), lambda bi, i: (bi, jnp.maximum(i * nh - 1, 0), 0)),
                  pl.BlockSpec((1, HALO, D_A), lambda bi, i: (bi, jnp.minimum((i + 1) * nh, last), 0)),
                  pl.BlockSpec((1, tt, D_B), lambda bi, i: (bi, i, 0)),
                  pl.BlockSpec((1, tt, 2 * d), lambda bi, i: (bi, i, 0)),
                  pl.BlockSpec((len(POOL_HALF), A_GROUP, A_GROUP), lambda bi, i: (0, 0, 0)),
                  row(D_A),
                  pl.BlockSpec((D_A, d), lambda bi, i: (0, 0)),
                  pl.BlockSpec((D_B, d), lambda bi, i: (0, 0)),
                  pl.BlockSpec((d, d), lambda bi, i: (0, 0)),
                  vec(), row(d), row(d), vec(), vec()],
        out_specs=[pl.BlockSpec((1, tt, d), lambda bi, i: (bi, i, 0)),
                   pl.BlockSpec((1, d, tt), lambda bi, i: (bi, 0, i))],
        out_shape=[jax.ShapeDtypeStruct((b, s, d), F32),
                   jax.ShapeDtypeStruct((b, d, s), BF16)],
        scratch_shapes=[pltpu.VMEM((tt + 2 * HALO, D_A), F32)],
        compiler_params=_params(("arbitrary", "arbitrary")),
        name="merge",
    )(x, a, a, a, ob, sg, pool_w, pool_scale, w_up_a, w_up_b, w_out, gate1, g1, b1, shift2, scale2)


def _bitonic_merge(xs, descending):
    n = len(xs)
    if n == 1:
        return xs
    h = n // 2
    big = [jnp.maximum(xs[i], xs[i + h]) for i in range(h)]
    small = [jnp.minimum(xs[i], xs[i + h]) for i in range(h)]
    first, second = (big, small) if descending else (small, big)
    return _bitonic_merge(first, descending) + _bitonic_merge(second, descending)


def _bitonic_sort(xs, descending=True):
    n = len(xs)
    if n == 1:
        return xs
    h = n // 2
    return _bitonic_merge(_bitonic_sort(xs[:h], True) + _bitonic_sort(xs[h:], False), descending)


def _merge_top(a, b):
    n = len(a)
    return _bitonic_merge([jnp.maximum(a[i], b[n - 1 - i]) for i in range(n)], True)


def _top_sorted(xs, k):
    groups = [_bitonic_sort(xs[g:g + k]) for g in range(0, len(xs), k)]
    while len(groups) > 1:
        groups = [_merge_top(groups[g], groups[g + 1]) for g in range(0, len(groups), 2)]
    return groups[0]


def _route_kernel(h2t_ref, kq_ref, a_ref, thr_ref, b_ref, s2_scr):
    tt = h2t_ref.shape[2]
    nk, nh = N_KEYS, PEER_HEADS
    half_rows = nk * nh
    neg = jnp.full((nh, LANES), NEG_INF, F32)
    per_piece = MXU_WIDTH // LANES
    piece = lambda p: jnp.dot(kq_ref[...], h2t_ref[0, :, p * MXU_WIDTH:(p + 1) * MXU_WIDTH],
                              preferred_element_type=F32)
    pieces = [piece(0)]

    for c in range(tt // LANES):
        lanes = slice(c * LANES, (c + 1) * LANES)
        if c % per_piece == 1 and len(pieces) < tt // MXU_WIDTH:
            pieces.append(piece(len(pieces)))
        s_all = pieces[c // per_piece][:, (c % per_piece) * LANES:(c % per_piece + 1) * LANES]
        s1 = [s_all[n * nh:(n + 1) * nh, :] for n in range(nk)]
        s2 = [s_all[half_rows + n * nh:half_rows + (n + 1) * nh, :] for n in range(nk)]
        v1 = _top_sorted(s1, TOPK)
        v2 = _top_sorted(s2, TOPK)

        row = lambda k1: [v1[k1] + v2[k2] for k2 in range(TOPK // (k1 + 1))]
        l0 = row(0)
        l1 = row(1) + [neg] * 8
        l2 = _bitonic_sort(row(2) + row(3) + row(4) + [neg] * 4)
        l3 = _bitonic_sort(sum([row(k1) for k1 in range(5, TOPK)], []) + [neg] * 2)
        tau = _merge_top(_merge_top(l0, l1), _merge_top(l2, l3))[TOPK - 1]

        t0 = tau - v2[0]
        e1v = [jnp.exp(v - v1[0]) for v in v1]
        e2v = [jnp.exp(v - v2[0]) for v in v2]
        thrv = [jnp.exp(t0 - v) for v in v1]
        z = jnp.zeros_like(tau)
        for (k1, k2) in _CAND:
            z = z + jnp.where(e2v[k2] >= thrv[k1], e1v[k1] * e2v[k2], 0.0)
        inv_z = 1.0 / z

        for n in range(nk):
            a_ref[0, n, :, lanes] = jnp.where(s1[n] >= v1[TOPK - 1], jnp.exp(s1[n] - v1[0]), 0.0) * inv_z
            thr_ref[0, n, :, lanes] = jnp.exp(t0 - s1[n])

        s2_scr[...] = s_all[half_rows:2 * half_rows, :]
        tau2 = v2[TOPK - 1]
        for hd in range(nh):
            s2h = s2_scr[pl.ds(hd, nk, stride=nh), :]
            b_ref[0, hd, :, lanes] = jnp.where(s2h >= tau2[hd:hd + 1, :],
                                               jnp.exp(s2h - v2[0][hd:hd + 1, :]), 0.0)


def _route(h2t, kq, tt):
    b, d, s = h2t.shape
    nk, nh = N_KEYS, PEER_HEADS
    full = lambda: pl.BlockSpec((d, d), lambda bi, i: (0, 0), pipeline_mode=pl.Buffered(1))
    eh = lambda: pl.BlockSpec((1, nk, nh, tt), lambda bi, i: (bi, 0, 0, i))
    he = lambda: pl.BlockSpec((1, nh, nk, tt), lambda bi, i: (bi, 0, 0, i))
    return pl.pallas_call(
        _route_kernel,
        grid=(b, s // tt),
        in_specs=[pl.BlockSpec((1, d, tt), lambda bi, i: (bi, 0, i)), full()],
        out_specs=[eh(), eh(), he()],
        out_shape=[jax.ShapeDtypeStruct((b, nk, nh, s), F32),
                   jax.ShapeDtypeStruct((b, nk, nh, s), F32),
                   jax.ShapeDtypeStruct((b, nh, nk, s), F32)],
        scratch_shapes=[pltpu.VMEM((nk * nh, LANES), F32)],
        compiler_params=_params(("arbitrary", "arbitrary")),
        name="route",
    )(h2t, kq)


def _experts_step(h2t_ref, u_ref, vt_ref, a_ref, thr_ref, b_ref, acc_scr, xw_scr, hu_next, hu_cur):
    te = u_ref.shape[0]
    tt = h2t_ref.shape[2]
    nsub = te // EXPERT_SUB
    for half in range(tt // EXPERT_PIECE):
        cols = slice(half * EXPERT_PIECE, (half + 1) * EXPERT_PIECE)
        for sub in range(nsub):
            rows = slice(sub * EXPERT_SUB, (sub + 1) * EXPERT_SUB)
            hu_next[rows, cols] = jnp.dot(u_ref[rows, :], h2t_ref[0, :, cols], preferred_element_type=F32)
            for l in range(EXPERT_SUB // N_KEYS):
                e1 = sub * (EXPERT_SUB // N_KEYS) + l
                for c in range(EXPERT_PIECE // LANES):
                    lo = half * EXPERT_PIECE + c * LANES
                    lanes = slice(lo, lo + LANES)
                    thr = [jnp.broadcast_to(thr_ref[0, e1, hd:hd + 1, lanes], (PACK_ROWS, LANES))
                           for hd in range(PEER_HEADS)]
                    a = [jnp.broadcast_to(a_ref[0, e1, hd:hd + 1, lanes], (PACK_ROWS, LANES))
                         for hd in range(PEER_HEADS)]
                    for r in range(N_KEYS // PACK_ROWS):
                        k2 = slice(r * PACK_ROWS, (r + 1) * PACK_ROWS)
                        w = None
                        for hd in range(PEER_HEADS):
                            bv = b_ref[0, hd, k2, lanes]
                            term = jnp.where(bv >= thr[hd], bv, 0.0) * a[hd]
                            w = term if w is None else w + term
                        k2s = slice(e1 * N_KEYS + r * PACK_ROWS, e1 * N_KEYS + (r + 1) * PACK_ROWS)
                        xw_scr[k2s, lanes] = (w * _gelu(hu_cur[k2s, lanes])).astype(BF16)
            acc_scr[:, cols] += jnp.dot(vt_ref[:, rows], xw_scr[rows, cols], preferred_element_type=F32)


def _experts_kernel(h2t_ref, u_ref, vt_ref, a_ref, thr_ref, b_ref, x1_ref, gate2_ref,
                    g2_ref, b2_ref, y_ref, acc_scr, hu_even, hu_odd, xw_scr, *, ne):
    s = pl.program_id(1)

    @pl.when(s == 0)
    def _():
        acc_scr[...] = jnp.zeros_like(acc_scr)
        hu_odd[...] = jnp.zeros_like(hu_odd)

    step = functools.partial(_experts_step, h2t_ref, u_ref, vt_ref, a_ref, thr_ref, b_ref,
                             acc_scr, xw_scr)

    @pl.when(s % 2 == 0)
    def _():
        step(hu_even, hu_odd)

    @pl.when(s % 2 == 1)
    def _():
        step(hu_odd, hu_even)

    @pl.when((s >= 1) & ((s - 1) % ne == ne - 1))
    def _():
        y2 = acc_scr[...].T
        y_ref[0] = _norm0(ALPHA * x1_ref[0] + gate2_ref[0] * y2) * g2_ref[...] + b2_ref[...]
        acc_scr[...] = jnp.zeros_like(acc_scr)


def _experts(h2t, u_bf, vt_bf, a, thr, bm, x1, gate2, g2, b2, tt, te):
    b, d, s = h2t.shape
    nk, nh = N_KEYS, PEER_HEADS
    nl = te // nk
    ni = s // tt
    ne = N_EXPERTS // te
    nxt_i = lambda st: jnp.minimum(st // ne, ni - 1)
    cur = lambda st: jnp.maximum(st - 1, 0)
    eh = lambda: pl.BlockSpec((1, nl, nh, tt), lambda bi, st: (bi, cur(st) % ne, 0, cur(st) // ne))
    he = lambda: pl.BlockSpec((1, nh, nk, tt), lambda bi, st: (bi, 0, 0, cur(st) // ne))
    row = lambda: pl.BlockSpec((1, d), lambda bi, st: (0, 0))
    return pl.pallas_call(
        functools.partial(_experts_kernel, ne=ne),
        grid=(b, ni * ne + 1),
        in_specs=[pl.BlockSpec((1, d, tt), lambda bi, st: (bi, 0, nxt_i(st))),
                  pl.BlockSpec((te, d), lambda bi, st: (st % ne, 0)),
                  pl.BlockSpec((d, te), lambda bi, st: (0, cur(st) % ne)),
                  eh(), eh(), he(),
                  pl.BlockSpec((1, tt, d), lambda bi, st: (bi, cur(st) // ne, 0)),
                  pl.BlockSpec((1, 1, d), lambda bi, st: (bi, 0, 0)),
                  row(), row()],
        out_specs=pl.BlockSpec((1, tt, d), lambda bi, st: (bi, cur(st) // ne, 0)),
        out_shape=jax.ShapeDtypeStruct((b, s, d), F32),
        scratch_shapes=[pltpu.VMEM((d, tt), F32),
                        pltpu.VMEM((te, tt), F32),
                        pltpu.VMEM((te, tt), F32),
                        pltpu.VMEM((te, tt), BF16)],
        compiler_params=_params(("arbitrary", "arbitrary")),
        name="experts",
    )(h2t, u_bf, vt_bf, a, thr, bm, x1, gate2, g2, b2)


def _block_diag_keys(keys):
    nh, _, nk, c = keys.shape
    eye = jnp.eye(nh, dtype=keys.dtype)
    full = jnp.einsum("hknc,hg,kj->knhgjc", keys, eye, jnp.eye(2, dtype=keys.dtype))
    return full.reshape(2 * nk * nh, nh * 2 * c)


def _trunk(x, mod, wts, tiles):
    b, s, d = x.shape
    shift1, scale1, gate1, shift2, scale2, gate2 = [mod[:, k][:, None, :] for k in range(6)]
    t_in, t_merge, t_route, t_exp, t_e = tiles
    a, ob, sg = _inproj(x, shift1, scale1, wts["w_in"], wts["sgu_ln_g"], wts["sgu_ln_b"],
                        wts["sgu_w"], wts["sgu_b_full"], min(t_in, s))
    x1, h2t = _merge(x, a, ob, sg, wts["pool_w"], wts["pool_scale"], wts["w_up_a"], wts["w_up_b"],
                     wts["w_out"], gate1, wts["ln1_g"], wts["ln1_b"], shift2, scale2, min(t_merge, s))
    ra, rthr, rb = _route(h2t, wts["kq"], min(t_route, s))
    return _experts(h2t, wts["u_bf"], wts["vt_bf"], ra, rthr, rb, x1, gate2,
                    wts["ln2_g"], wts["ln2_b"], min(t_exp, s), t_e)


def _prepare(w_ada, b_ada, w_in, pool_w, pool_scale, sgu_ln_g, sgu_ln_b, sgu_w, sgu_b, w_up_a, w_up_b,
             w_out, ln1_g, ln1_b, peer_wq, peer_keys, peer_u, peer_v, ln2_g, ln2_b):
    l = 0
    return {
        "w_in": w_in[l].astype(BF16),
        "pool_w": pool_w[l].astype(BF16),
        "pool_scale": pool_scale[l][None, :],
        "sgu_ln_g": sgu_ln_g[l][None, :],
        "sgu_ln_b": sgu_ln_b[l][None, :],
        "sgu_w": sgu_w[l].astype(BF16),
        "sgu_b_full": jnp.broadcast_to(sgu_b[l][:, :, None], (SGU_HEADS, CHUNK, CHUNK)),
        "w_up_a": w_up_a[l].astype(BF16),
        "w_up_b": w_up_b[l].astype(BF16),
        "w_out": w_out[l].astype(BF16),
        "ln1_g": ln1_g[l][None, :],
        "ln1_b": ln1_b[l][None, :],
        "kq": _fold(_block_diag_keys(peer_keys[l]).astype(BF16), _transpose_cast(peer_wq[l])),
        "u_bf": peer_u[l].astype(BF16),
        "vt_bf": _transpose_cast(peer_v[l]),
        "ln2_g": ln2_g[l][None, :],
        "ln2_b": ln2_b[l][None, :],
    }


_TILES = (256, 256, 512, 512, 1024)


def kernel(x_prompt, x_sample, c_prompt, c_sample, w_ada, b_ada, w_in, pool_w, pool_scale, sgu_ln_g,
           sgu_ln_b, sgu_w, sgu_b, w_up_a, w_up_b, w_out, ln1_g, ln1_b, peer_wq, peer_keys, peer_u,
           peer_v, ln2_g, ln2_b):
    assert w_ada.shape[0] == DEPTH == 1
    bp = x_prompt.shape[0]
    bs = x_sample.shape[0]
    c_all = jnp.concatenate([c_prompt, c_sample], axis=0)
    c8 = jnp.pad(c_all, ((0, 8 - bp - bs), (0, 0)))
    mod = _ada(c8, w_ada[0], b_ada[0][None, :]).reshape(8, 6, D_MODEL)
    wts = _prepare(w_ada, b_ada, w_in, pool_w, pool_scale, sgu_ln_g, sgu_ln_b, sgu_w, sgu_b, w_up_a,
                   w_up_b, w_out, ln1_g, ln1_b, peer_wq, peer_keys, peer_u, peer_v, ln2_g, ln2_b)
    y_prompt = _trunk(x_prompt, mod[:bp], wts, _TILES)
    y_sample = _trunk(x_sample, mod[bp:bp + bs], wts, _TILES)
    return (y_prompt, y_sample)
```

```python
import functools
import math

import jax
import jax.numpy as jnp
from jax import lax
from jax.experimental import pallas as pl
from jax.experimental.pallas import tpu as pltpu

F32 = jnp.float32
BF16 = jnp.bfloat16

D_MODEL = 2048
D_A = 1024
D_B = 1024
POOL_HALF = (1, 2, 4, 8)
A_GROUP = 256
CHUNK = 128
SGU_HEADS = 8
PEER_HEADS = 8
N_KEYS = 128
N_EXPERTS = N_KEYS * N_KEYS
TOPK = 16
EXPERT_SUB = 256
DEPTH = 1
ALPHA = (2.0 * DEPTH) ** 0.25
LN_EPS = 1e-5
HALO = 8
LANES = 128
MXU_WIDTH = 256
EXPERT_PIECE = MXU_WIDTH
INPROJ_TILE = 1024
PREP_TILE = 512
MERGE_SUB = 256
ADA_TILE = 1024
PACK_ROWS = 16
NEG_INF = float("-inf")
VMEM_LIMIT = 60 * 1024 * 1024

_CAND = tuple((k1, k2) for k1 in range(TOPK) for k2 in range(TOPK) if (k1 + 1) * (k2 + 1) <= TOPK)


def _gelu(x):
    c = math.sqrt(2.0 / math.pi)
    return x * (0.5 * (1.0 + jnp.tanh(c * (x + 0.044715 * (x * x * x)))))


def _sigmoid(x):
    return 1.0 / (1.0 + jnp.exp(-x))


def _norm0(x):
    mu = jnp.mean(x, axis=-1, keepdims=True)
    xc = x - mu
    var = jnp.mean(xc * xc, axis=-1, keepdims=True)
    return xc * lax.rsqrt(var + LN_EPS)


def _params(sem):
    return pltpu.CompilerParams(dimension_semantics=sem, vmem_limit_bytes=VMEM_LIMIT)


def _ada_kernel(c_ref, w_ref, b_ref, o_ref):
    c = c_ref[...]
    sc = c * _sigmoid(c)
    o_ref[...] = jnp.dot(sc.astype(BF16), w_ref[...].astype(BF16),
                         preferred_element_type=F32) + b_ref[...]


def _ada(c8, w_ada, b_ada):
    n = w_ada.shape[1]
    tn = ADA_TILE
    return pl.pallas_call(
        _ada_kernel,
        grid=(n // tn,),
        in_specs=[pl.BlockSpec((8, D_MODEL), lambda j: (0, 0)),
                  pl.BlockSpec((D_MODEL, tn), lambda j: (0, j)),
                  pl.BlockSpec((1, tn), lambda j: (0, j))],
        out_specs=pl.BlockSpec((8, tn), lambda j: (0, j)),
        out_shape=jax.ShapeDtypeStruct((8, n), F32),
        compiler_params=_params(("arbitrary",)),
        name="ada",
    )(c8, w_ada, b_ada)


def _tcast_kernel(x_ref, o_ref):
    o_ref[...] = x_ref[...].T.astype(BF16)


def _transpose_cast(x):
    r, c = x.shape
    tr = PREP_TILE
    return pl.pallas_call(
        _tcast_kernel,
        grid=(r // tr,),
        in_specs=[pl.BlockSpec((tr, c), lambda i: (i, 0))],
        out_specs=pl.BlockSpec((c, tr), lambda i: (0, i)),
        out_shape=jax.ShapeDtypeStruct((c, r), BF16),
        compiler_params=_params(("arbitrary",)),
        name="transpose_cast",
    )(x)


def _fold_kernel(a_ref, b_ref, o_ref):
    o_ref[...] = jnp.dot(a_ref[...], b_ref[...], preferred_element_type=F32).astype(BF16)


def _fold(a, b):
    m, k = a.shape
    n = b.shape[1]
    tn = PREP_TILE
    return pl.pallas_call(
        _fold_kernel,
        grid=(n // tn,),
        in_specs=[pl.BlockSpec((m, k), lambda j: (0, 0)),
                  pl.BlockSpec((k, tn), lambda j: (0, j))],
        out_specs=pl.BlockSpec((m, tn), lambda j: (0, j)),
        out_shape=jax.ShapeDtypeStruct((m, n), BF16),
        compiler_params=_params(("arbitrary",)),
        name="fold_keys",
    )(a, b)


def _inproj_kernel(x_ref, shift_ref, scale_ref, w_ref, lng_ref, lnb_ref, ws_ref, bs_ref,
                   a_ref, ob_ref, sg_ref):
    tt = x_ref.shape[1]
    d = x_ref.shape[2]
    h = (_norm0(x_ref[0]) * (1.0 + scale_ref[0]) + shift_ref[0]).astype(BF16)
    proj = lambda lo, hi: jnp.dot(h, w_ref[:, lo:hi], preferred_element_type=F32)
    a_ref[0] = proj(0, D_A)
    gu = _gelu(proj(D_A, D_A + D_B)).astype(BF16)
    vn = (_norm0(_gelu(proj(D_A + D_B, D_A + 2 * D_B))) * lng_ref[...] + lnb_ref[...]).astype(BF16)
    for c in range(tt // CHUNK):
        rows = slice(c * CHUNK, (c + 1) * CHUNK)
        for hd in range(SGU_HEADS):
            cols = slice(hd * CHUNK, (hd + 1) * CHUNK)
            mixed = jnp.dot(ws_ref[hd], vn[rows, cols], preferred_element_type=F32) + bs_ref[hd]
            ob_ref[0, rows, cols] = (gu[rows, cols].astype(F32) * mixed).astype(BF16)
    g0 = D_A + 2 * D_B
    for t in range(2 * d // INPROJ_TILE):
        cols = slice(t * INPROJ_TILE, (t + 1) * INPROJ_TILE)
        sg_ref[0, :, cols] = _sigmoid(proj(g0 + t * INPROJ_TILE, g0 + (t + 1) * INPROJ_TILE)).astype(BF16)


def _inproj(x, shift1, scale1, w_in, lng, lnb, ws, bs_full, tt):
    b, s, d = x.shape
    n = w_in.shape[1]
    const = lambda shape: pl.BlockSpec(shape, lambda bi, i: (0,) * len(shape), pipeline_mode=pl.Buffered(1))
    return pl.pallas_call(
        _inproj_kernel,
        grid=(b, s // tt),
        in_specs=[pl.BlockSpec((1, tt, d), lambda bi, i: (bi, i, 0)),
                  pl.BlockSpec((1, 1, d), lambda bi, i: (bi, 0, 0)),
                  pl.BlockSpec((1, 1, d), lambda bi, i: (bi, 0, 0)),
                  const((d, n)),
                  const((1, D_B)),
                  const((1, D_B)),
                  const((SGU_HEADS, CHUNK, CHUNK)),
                  const((SGU_HEADS, CHUNK, CHUNK))],
        out_specs=[pl.BlockSpec((1, tt, D_A), lambda bi, i: (bi, i, 0)),
                   pl.BlockSpec((1, tt, D_B), lambda bi, i: (bi, i, 0)),
                   pl.BlockSpec((1, tt, 2 * d), lambda bi, i: (bi, i, 0))],
        out_shape=[jax.ShapeDtypeStruct((b, s, D_A), F32),
                   jax.ShapeDtypeStruct((b, s, D_B), BF16),
                   jax.ShapeDtypeStruct((b, s, 2 * d), BF16)],
        compiler_params=_params(("arbitrary", "arbitrary")),
        name="inproj",
    )(x, shift1, scale1, w_in, lng, lnb, ws, bs_full)


def _merge_kernel(x_ref, a_ref, ap_ref, an_ref, ob_ref, sg_ref, pw_ref, ps_ref, wua_ref, wub_ref,
                  wo_ref, gate1_ref, g1_ref, b1_ref, shift2_ref, scale2_ref,
                  x1_ref, h2t_ref, ext_scr, *, seq_len):
    i = pl.program_id(1)
    ni = pl.num_programs(1)
    tt = x_ref.shape[1]
    d = x_ref.shape[2]

    ext_scr[0:HALO, :] = jnp.where(i > 0, ap_ref[0], 0.0)
    ext_scr[HALO:HALO + tt, :] = a_ref[0]
    ext_scr[HALO + tt:HALO + tt + HALO, :] = jnp.where(i < ni - 1, an_ref[0], 0.0)

    for r0 in range(0, tt, MERGE_SUB):
        rows = slice(r0, r0 + MERGE_SUB)
        pos = i * tt + r0 + lax.broadcasted_iota(jnp.int32, (MERGE_SUB, A_GROUP), 0)
        ya = jnp.zeros((MERGE_SUB, d), F32)
        for g, hw in enumerate(POOL_HALF):
            cols = slice(g * A_GROUP, (g + 1) * A_GROUP)
            base = HALO + r0
            win = ext_scr[base - hw:base - hw + MERGE_SUB, cols]
            for off in range(-hw + 1, hw):
                win = win + ext_scr[base + off:base + off + MERGE_SUB, cols]
            cnt = (jnp.minimum(pos + hw, seq_len) - jnp.maximum(pos - hw, 0)).astype(F32)
            pooled = win / cnt - ext_scr[base:base + MERGE_SUB, cols]
            mixed = jnp.dot(pooled.astype(BF16), pw_ref[g], preferred_element_type=F32) * ps_ref[:, cols]
            ya = ya + jnp.dot(mixed.astype(BF16), wua_ref[cols, :], preferred_element_type=F32)

        yb = jnp.dot(ob_ref[0, rows, :], wub_ref[...], preferred_element_type=F32)
        merged = sg_ref[0, rows, 0:d].astype(F32) * ya + sg_ref[0, rows, d:2 * d].astype(F32) * yb
        y = jnp.dot(merged.astype(BF16), wo_ref[...], preferred_element_type=F32)
        x1 = _norm0(ALPHA * x_ref[0, rows, :] + gate1_ref[0] * y) * g1_ref[...] + b1_ref[...]
        x1_ref[0, rows, :] = x1
        h2 = _norm0(x1) * (1.0 + scale2_ref[0]) + shift2_ref[0]
        h2t_ref[0, :, rows] = h2.T.astype(BF16)


def _merge(x, a, ob, sg, pool_w, pool_scale, w_up_a, w_up_b, w_out, gate1, g1, b1, shift2, scale2, tt):
    b, s, d = x.shape
    nh = tt // HALO
    last = s // HALO - 1
    vec = lambda: pl.BlockSpec((1, 1, d), lambda bi, i: (bi, 0, 0))
    row = lambda n: pl.BlockSpec((1, n), lambda bi, i: (0, 0))
    return pl.pallas_call(
        functools.partial(_merge_kernel, seq_len=s),
        grid=(b, s // tt),
        in_specs=[pl.BlockSpec((1, tt, d), lambda bi, i: (bi, i, 0)),
                  pl.BlockSpec((1, tt, D_A), lambda bi, i: (bi, i, 0)),
                  pl.BlockSpec((1, HALO, D_A), lambda bi, i: (bi, jnp.maximum(i * nh - 1, 0), 0)),
                  pl.BlockSpec((1, HALO, D_A), lambda bi, i: (bi, jnp.minimum((i + 1) * nh, last), 0)),
                  pl.BlockSpec((1, tt, D_B), lambda bi, i: (bi, i, 0)),
                  pl.BlockSpec((1, tt, 2 * d), lambda bi, i: (bi, i, 0)),
                  pl.BlockSpec((len(POOL_HALF), A_GROUP, A_GROUP), lambda bi, i: (0, 0, 0)),
                  row(D_A),
                  pl.BlockSpec((D_A, d), lambda bi, i: (0, 0), pipeline_mode=pl.Buffered(1)),
                  pl.BlockSpec((D_B, d), lambda bi, i: (0, 0), pipeline_mode=pl.Buffered(1)),
                  pl.BlockSpec((d, d), lambda bi, i: (0, 0), pipeline_mode=pl.Buffered(1)),
                  vec(), row(d), row(d), vec(), vec()],
        out_specs=[pl.BlockSpec((1, tt, d), lambda bi, i: (bi, i, 0)),
                   pl.BlockSpec((1, d, tt), lambda bi, i: (bi, 0, i))],
        out_shape=[jax.ShapeDtypeStruct((b, s, d), F32),
                   jax.ShapeDtypeStruct((b, d, s), BF16)],
        scratch_shapes=[pltpu.VMEM((tt + 2 * HALO, D_A), F32)],
        compiler_params=_params(("arbitrary", "arbitrary")),
        name="merge",
    )(x, a, a, a, ob, sg, pool_w, pool_scale, w_up_a, w_up_b, w_out, gate1, g1, b1, shift2, scale2)


def _bitonic_merge(xs, descending):
    n = len(xs)
    if n == 1:
        return xs
    h = n // 2
    big = [jnp.maximum(xs[i], xs[i + h]) for i in range(h)]
    small = [jnp.minimum(xs[i], xs[i + h]) for i in range(h)]
    first, second = (big, small) if descending else (small, big)
    return _bitonic_merge(first, descending) + _bitonic_merge(second, descending)


def _bitonic_sort(xs, descending=True):
    n = len(xs)
    if n == 1:
        return xs
    h = n // 2
    return _bitonic_merge(_bitonic_sort(xs[:h], True) + _bitonic_sort(xs[h:], False), descending)


def _merge_top(a, b):
    n = len(a)
    return _bitonic_merge([jnp.maximum(a[i], b[n - 1 - i]) for i in range(n)], True)


def _top_sorted(xs, k):
    groups = [_bitonic_sort(xs[g:g + k]) for g in range(0, len(xs), k)]
    while len(groups) > 1:
        groups = [_merge_top(groups[g], groups[g + 1]) for g in range(0, len(groups), 2)]
    return groups[0]


def _route_kernel(h2t_ref, kq_ref, a_ref, thr_ref, b_ref, s2_scr):
    tt = h2t_ref.shape[2]
    nk, nh = N_KEYS, PEER_HEADS
    half_rows = nk * nh
    neg = jnp.full((nh, LANES), NEG_INF, F32)
    per_piece = MXU_WIDTH // LANES
    piece = lambda p: jnp.dot(kq_ref[...], h2t_ref[0, :, p * MXU_WIDTH:(p + 1) * MXU_WIDTH],
                              preferred_element_type=F32)
    pieces = [piece(0)]

    for c in range(tt // LANES):
        lanes = slice(c * LANES, (c + 1) * LANES)
        if c % per_piece == 1 and len(pieces) < tt // MXU_WIDTH:
            pieces.append(piece(len(pieces)))
        s_all = pieces[c // per_piece][:, (c % per_piece) * LANES:(c % per_piece + 1) * LANES]
        s1 = [s_all[n * nh:(n + 1) * nh, :] for n in range(nk)]
        s2 = [s_all[half_rows + n * nh:half_rows + (n + 1) * nh, :] for n in range(nk)]
        v1 = _top_sorted(s1, TOPK)
        v2 = _top_sorted(s2, TOPK)

        row = lambda k1: [v1[k1] + v2[k2] for k2 in range(TOPK // (k1 + 1))]
        l0 = row(0)
        l1 = row(1) + [neg] * 8
        l2 = _bitonic_sort(row(2) + row(3) + row(4) + [neg] * 4)
        l3 = _bitonic_sort(sum([row(k1) for k1 in range(5, TOPK)], []) + [neg] * 2)
        tau = _merge_top(_merge_top(l0, l1), _merge_top(l2, l3))[TOPK - 1]

        t0 = tau - v2[0]
        e1v = [jnp.exp(v - v1[0]) for v in v1]
        e2v = [jnp.exp(v - v2[0]) for v in v2]
        thrv = [jnp.exp(t0 - v) for v in v1]
        z = jnp.zeros_like(tau)
        for (k1, k2) in _CAND:
            z = z + jnp.where(e2v[k2] >= thrv[k1], e1v[k1] * e2v[k2], 0.0)
        inv_z = 1.0 / z

        for n in range(nk):
            a_ref[0, n, :, lanes] = jnp.where(s1[n] >= v1[TOPK - 1], jnp.exp(s1[n] - v1[0]), 0.0) * inv_z
            thr_ref[0, n, :, lanes] = jnp.exp(t0 - s1[n])

        s2_scr[...] = s_all[half_rows:2 * half_rows, :]
        tau2 = v2[TOPK - 1]
        for hd in range(nh):
            s2h = s2_scr[pl.ds(hd, nk, stride=nh), :]
            b_ref[0, hd, :, lanes] = jnp.where(s2h >= tau2[hd:hd + 1, :],
                                               jnp.exp(s2h - v2[0][hd:hd + 1, :]), 0.0)


def _route(h2t, kq, tt):
    b, d, s = h2t.shape
    nk, nh = N_KEYS, PEER_HEADS
    full = lambda: pl.BlockSpec((d, d), lambda bi, i: (0, 0), pipeline_mode=pl.Buffered(1))
    eh = lambda: pl.BlockSpec((1, nk, nh, tt), lambda bi, i: (bi, 0, 0, i))
    he = lambda: pl.BlockSpec((1, nh, nk, tt), lambda bi, i: (bi, 0, 0, i))
    return pl.pallas_call(
        _route_kernel,
        grid=(b, s // tt),
        in_specs=[pl.BlockSpec((1, d, tt), lambda bi, i: (bi, 0, i)), full()],
        out_specs=[eh(), eh(), he()],
        out_shape=[jax.ShapeDtypeStruct((b, nk, nh, s), F32),
                   jax.ShapeDtypeStruct((b, nk, nh, s), F32),
                   jax.ShapeDtypeStruct((b, nh, nk, s), F32)],
        scratch_shapes=[pltpu.VMEM((nk * nh, LANES), F32)],
        compiler_params=_params(("arbitrary", "arbitrary")),
        name="route",
    )(h2t, kq)


def _experts_step(h2t_ref, u_ref, vt_ref, a_ref, thr_ref, b_ref, acc_scr, xw_scr, hu_next, hu_cur):
    te = u_ref.shape[0]
    tt = h2t_ref.shape[2]
    nsub = te // EXPERT_SUB
    for half in range(tt // EXPERT_PIECE):
        cols = slice(half * EXPERT_PIECE, (half + 1) * EXPERT_PIECE)
        for sub in range(nsub):
            rows = slice(sub * EXPERT_SUB, (sub + 1) * EXPERT_SUB)
            hu_next[rows, cols] = jnp.dot(u_ref[rows, :], h2t_ref[0, :, cols], preferred_element_type=F32)
            for l in range(EXPERT_SUB // N_KEYS):
                e1 = sub * (EXPERT_SUB // N_KEYS) + l
                for c in range(EXPERT_PIECE // LANES):
                    lo = half * EXPERT_PIECE + c * LANES
                    lanes = slice(lo, lo + LANES)
                    thr = [jnp.broadcast_to(thr_ref[0, e1, hd:hd + 1, lanes], (PACK_ROWS, LANES))
                           for hd in range(PEER_HEADS)]
                    a = [jnp.broadcast_to(a_ref[0, e1, hd:hd + 1, lanes], (PACK_ROWS, LANES))
                         for hd in range(PEER_HEADS)]
                    for r in range(N_KEYS // PACK_ROWS):
                        k2 = slice(r * PACK_ROWS, (r + 1) * PACK_ROWS)
                        w = None
                        for hd in range(PEER_HEADS):
                            bv = b_ref[0, hd, k2, lanes]
                            term = jnp.where(bv >= thr[hd], bv, 0.0) * a[hd]
                            w = term if w is None else w + term
                        k2s = slice(e1 * N_KEYS + r * PACK_ROWS, e1 * N_KEYS + (r + 1) * PACK_ROWS)
                        xw_scr[k2s, lanes] = (w * _gelu(hu_cur[k2s, lanes])).astype(BF16)
            acc_scr[:, cols] += jnp.dot(vt_ref[:, rows], xw_scr[rows, cols], preferred_element_type=F32)


def _experts_kernel(h2t_ref, u_ref, vt_ref, a_ref, thr_ref, b_ref, x1_ref, gate2_ref,
                    g2_ref, b2_ref, y_ref, acc_scr, hu_even, hu_odd, xw_scr, *, ne):
    s = pl.program_id(1)

    @pl.when(s == 0)
    def _():
        acc_scr[...] = jnp.zeros_like(acc_scr)
        hu_odd[...] = jnp.zeros_like(hu_odd)

    step = functools.partial(_experts_step, h2t_ref, u_ref, vt_ref, a_ref, thr_ref, b_ref,
                             acc_scr, xw_scr)

    @pl.when(s % 2 == 0)
    def _():
        step(hu_even, hu_odd)

    @pl.when(s % 2 == 1)
    def _():
        step(hu_odd, hu_even)

    @pl.when((s >= 1) & ((s - 1) % ne == ne - 1))
    def _():
        y2 = acc_scr[...].T
        y_ref[0] = _norm0(ALPHA * x1_ref[0] + gate2_ref[0] * y2) * g2_ref[...] + b2_ref[...]
        acc_scr[...] = jnp.zeros_like(acc_scr)


def _experts(h2t, u_bf, vt_bf, a, thr, bm, x1, gate2, g2, b2, tt, te):
    b, d, s = h2t.shape
    nk, nh = N_KEYS, PEER_HEADS
    nl = te // nk
    ni = s // tt
    ne = N_EXPERTS // te
    nxt_i = lambda st: jnp.minimum(st // ne, ni - 1)
    cur = lambda st: jnp.maximum(st - 1, 0)
    eh = lambda: pl.BlockSpec((1, nl, nh, tt), lambda bi, st: (bi, cur(st) % ne, 0, cur(st) // ne))
    he = lambda: pl.BlockSpec((1, nh, nk, tt), lambda bi, st: (bi, 0, 0, cur(st) // ne))
    row = lambda: pl.BlockSpec((1, d), lambda bi, st: (0, 0))
    return pl.pallas_call(
        functools.partial(_experts_kernel, ne=ne),
        grid=(b, ni * ne + 1),
        in_specs=[pl.BlockSpec((1, d, tt), lambda bi, st: (bi, 0, nxt_i(st))),
                  pl.BlockSpec((te, d), lambda bi, st: (st % ne, 0)),
                  pl.BlockSpec((d, te), lambda bi, st: (0, cur(st) % ne)),
                  eh(), eh(), he(),
                  pl.BlockSpec((1, tt, d), lambda bi, st: (bi, cur(st) // ne, 0)),
                  pl.BlockSpec((1, 1, d), lambda bi, st: (bi, 0, 0)),
                  row(), row()],
        out_specs=pl.BlockSpec((1, tt, d), lambda bi, st: (bi, cur(st) // ne, 0)),
        out_shape=jax.ShapeDtypeStruct((b, s, d), F32),
        scratch_shapes=[pltpu.VMEM((d, tt), F32),
                        pltpu.VMEM((te, tt), F32),
                        pltpu.VMEM((te, tt), F32),
                        pltpu.VMEM((te, tt), BF16)],
        compiler_params=_params(("arbitrary", "arbitrary")),
        name="experts",
    )(h2t, u_bf, vt_bf, a, thr, bm, x1, gate2, g2, b2)


def _block_diag_keys(keys):
    nh, _, nk, c = keys.shape
    eye = jnp.eye(nh, dtype=keys.dtype)
    full = jnp.einsum("hknc,hg,kj->knhgjc", keys, eye, jnp.eye(2, dtype=keys.dtype))
    return full.reshape(2 * nk * nh, nh * 2 * c)


def _trunk(x, mod, wts, tiles):
    b, s, d = x.shape
    shift1, scale1, gate1, shift2, scale2, gate2 = [mod[:, k][:, None, :] for k in range(6)]
    t_in, t_merge, t_route, t_exp, t_e = tiles
    a, ob, sg = _inproj(x, shift1, scale1, wts["w_in"], wts["sgu_ln_g"], wts["sgu_ln_b"],
                        wts["sgu_w"], wts["sgu_b_full"], min(t_in, s))
    x1, h2t = _merge(x, a, ob, sg, wts["pool_w"], wts["pool_scale"], wts["w_up_a"], wts["w_up_b"],
                     wts["w_out"], gate1, wts["ln1_g"], wts["ln1_b"], shift2, scale2, min(t_merge, s))
    ra, rthr, rb = _route(h2t, wts["kq"], min(t_route, s))
    return _experts(h2t, wts["u_bf"], wts["vt_bf"], ra, rthr, rb, x1, gate2,
                    wts["ln2_g"], wts["ln2_b"], min(t_exp, s), t_e)


def _prepare(w_ada, b_ada, w_in, pool_w, pool_scale, sgu_ln_g, sgu_ln_b, sgu_w, sgu_b, w_up_a, w_up_b,
             w_out, ln1_g, ln1_b, peer_wq, peer_keys, peer_u, peer_v, ln2_g, ln2_b):
    l = 0
    return {
        "w_in": w_in[l].astype(BF16),
        "pool_w": pool_w[l].astype(BF16),
        "pool_scale": pool_scale[l][None, :],
        "sgu_ln_g": sgu_ln_g[l][None, :],
        "sgu_ln_b": sgu_ln_b[l][None, :],
        "sgu_w": sgu_w[l].astype(BF16),
        "sgu_b_full": jnp.broadcast_to(sgu_b[l][:, :, None], (SGU_HEADS, CHUNK, CHUNK)),
        "w_up_a": w_up_a[l].astype(BF16),
        "w_up_b": w_up_b[l].astype(BF16),
        "w_out": w_out[l].astype(BF16),
        "ln1_g": ln1_g[l][None, :],
        "ln1_b": ln1_b[l][None, :],
        "kq": _fold(_block_diag_keys(peer_keys[l]).astype(BF16), _transpose_cast(peer_wq[l])),
        "u_bf": peer_u[l].astype(BF16),
        "vt_bf": _transpose_cast(peer_v[l]),
        "ln2_g": ln2_g[l][None, :],
        "ln2_b": ln2_b[l][None, :],
    }


_TILES = (256, 512, 512, 512, 1024)


def kernel(x_prompt, x_sample, c_prompt, c_sample, w_ada, b_ada, w_in, pool_w, pool_scale, sgu_ln_g,
           sgu_ln_b, sgu_w, sgu_b, w_up_a, w_up_b, w_out, ln1_g, ln1_b, peer_wq, peer_keys, peer_u,
           peer_v, ln2_g, ln2_b):
    assert w_ada.shape[0] == DEPTH == 1
    bp = x_prompt.shape[0]
    bs = x_sample.shape[0]
    c_all = jnp.concatenate([c_prompt, c_sample], axis=0)
    c8 = jnp.pad(c_all, ((0, 8 - bp - bs), (0, 0)))
    mod = _ada(c8, w_ada[0], b_ada[0][None, :]).reshape(8, 6, D_MODEL)
    wts = _prepare(w_ada, b_ada, w_in, pool_w, pool_scale, sgu_ln_g, sgu_ln_b, sgu_w, sgu_b, w_up_a,
                   w_up_b, w_out, ln1_g, ln1_b, peer_wq, peer_keys, peer_u, peer_v, ln2_g, ln2_b)
    y_prompt = _trunk(x_prompt, mod[:bp], wts, _TILES)
    y_sample = _trunk(x_sample, mod[bp:bp + bs], wts, _TILES)
    return (y_prompt, y_sample)
```

```python
import functools
import math

import jax
import jax.numpy as jnp
from jax import lax
from jax.experimental import pallas as pl
from jax.experimental.pallas import tpu as pltpu

F32 = jnp.float32
BF16 = jnp.bfloat16

D_MODEL = 2048
D_A = 1024
D_B = 1024
POOL_HALF = (1, 2, 4, 8)
A_GROUP = 256
CHUNK = 128
SGU_HEADS = 8
PEER_HEADS = 8
N_KEYS = 128
N_EXPERTS = N_KEYS * N_KEYS
TOPK = 16
EXPERT_SUB = 256
DEPTH = 1
ALPHA = (2.0 * DEPTH) ** 0.25
LN_EPS = 1e-5
HALO = 8
LANES = 128
MXU_WIDTH = 256
EXPERT_PIECE = MXU_WIDTH
INPROJ_TILE = 1024
PREP_TILE = 512
MERGE_SUB = 256
INPROJ_SUB = 256
ADA_TILE = 1024
PACK_ROWS = 16
NEG_INF = float("-inf")
VMEM_LIMIT = 60 * 1024 * 1024

_CAND = tuple((k1, k2) for k1 in range(TOPK) for k2 in range(TOPK) if (k1 + 1) * (k2 + 1) <= TOPK)


def _gelu(x):
    c = math.sqrt(2.0 / math.pi)
    return x * (0.5 * (1.0 + jnp.tanh(c * (x + 0.044715 * (x * x * x)))))


def _sigmoid(x):
    return 1.0 / (1.0 + jnp.exp(-x))


def _norm0(x):
    mu = jnp.mean(x, axis=-1, keepdims=True)
    xc = x - mu
    var = jnp.mean(xc * xc, axis=-1, keepdims=True)
    return xc * lax.rsqrt(var + LN_EPS)


def _params(sem):
    return pltpu.CompilerParams(dimension_semantics=sem, vmem_limit_bytes=VMEM_LIMIT)


def _ada_kernel(c_ref, w_ref, b_ref, o_ref):
    c = c_ref[...]
    sc = c * _sigmoid(c)
    o_ref[...] = jnp.dot(sc.astype(BF16), w_ref[...].astype(BF16),
                         preferred_element_type=F32) + b_ref[...]


def _ada(c8, w_ada, b_ada):
    n = w_ada.shape[1]
    tn = ADA_TILE
    return pl.pallas_call(
        _ada_kernel,
        grid=(n // tn,),
        in_specs=[pl.BlockSpec((8, D_MODEL), lambda j: (0, 0)),
                  pl.BlockSpec((D_MODEL, tn), lambda j: (0, j)),
                  pl.BlockSpec((1, tn), lambda j: (0, j))],
        out_specs=pl.BlockSpec((8, tn), lambda j: (0, j)),
        out_shape=jax.ShapeDtypeStruct((8, n), F32),
        compiler_params=_params(("arbitrary",)),
        name="ada",
    )(c8, w_ada, b_ada)


def _tcast_kernel(x_ref, o_ref):
    o_ref[...] = x_ref[...].T.astype(BF16)


def _transpose_cast(x):
    r, c = x.shape
    tr = PREP_TILE
    return pl.pallas_call(
        _tcast_kernel,
        grid=(r // tr,),
        in_specs=[pl.BlockSpec((tr, c), lambda i: (i, 0))],
        out_specs=pl.BlockSpec((c, tr), lambda i: (0, i)),
        out_shape=jax.ShapeDtypeStruct((c, r), BF16),
        compiler_params=_params(("arbitrary",)),
        name="transpose_cast",
    )(x)


def _fold_kernel(a_ref, b_ref, o_ref):
    o_ref[...] = jnp.dot(a_ref[...], b_ref[...], preferred_element_type=F32).astype(BF16)


def _fold(a, b):
    m, k = a.shape
    n = b.shape[1]
    tn = PREP_TILE
    return pl.pallas_call(
        _fold_kernel,
        grid=(n // tn,),
        in_specs=[pl.BlockSpec((m, k), lambda j: (0, 0)),
                  pl.BlockSpec((k, tn), lambda j: (0, j))],
        out_specs=pl.BlockSpec((m, tn), lambda j: (0, j)),
        out_shape=jax.ShapeDtypeStruct((m, n), BF16),
        compiler_params=_params(("arbitrary",)),
        name="fold_keys",
    )(a, b)


def _inproj_kernel(x_ref, shift_ref, scale_ref, w_ref, lng_ref, lnb_ref, ws_ref, bs_ref,
                   a_ref, ob_ref, sg_ref):
    tt = x_ref.shape[1]
    d = x_ref.shape[2]
    for r0 in range(0, tt, INPROJ_SUB):
        rsub = slice(r0, r0 + INPROJ_SUB)
        h = (_norm0(x_ref[0, rsub, :]) * (1.0 + scale_ref[0]) + shift_ref[0]).astype(BF16)
        proj = lambda lo, hi: jnp.dot(h, w_ref[:, lo:hi], preferred_element_type=F32)
        a_ref[0, rsub, :] = proj(0, D_A)
        gu = _gelu(proj(D_A, D_A + D_B)).astype(BF16)
        vn = (_norm0(_gelu(proj(D_A + D_B, D_A + 2 * D_B))) * lng_ref[...] + lnb_ref[...]).astype(BF16)
        for c in range(INPROJ_SUB // CHUNK):
            rows = slice(c * CHUNK, (c + 1) * CHUNK)
            orow = slice(r0 + c * CHUNK, r0 + (c + 1) * CHUNK)
            for hd in range(SGU_HEADS):
                cols = slice(hd * CHUNK, (hd + 1) * CHUNK)
                mixed = jnp.dot(ws_ref[hd], vn[rows, cols], preferred_element_type=F32) + bs_ref[hd]
                ob_ref[0, orow, cols] = (gu[rows, cols].astype(F32) * mixed).astype(BF16)
        g0 = D_A + 2 * D_B
        for t in range(2 * d // INPROJ_TILE):
            cols = slice(t * INPROJ_TILE, (t + 1) * INPROJ_TILE)
            sg_ref[0, rsub, cols] = _sigmoid(proj(g0 + t * INPROJ_TILE, g0 + (t + 1) * INPROJ_TILE)).astype(BF16)


def _inproj(x, shift1, scale1, w_in, lng, lnb, ws, bs_full, tt):
    b, s, d = x.shape
    n = w_in.shape[1]
    const = lambda shape: pl.BlockSpec(shape, lambda bi, i: (0,) * len(shape), pipeline_mode=pl.Buffered(1))
    return pl.pallas_call(
        _inproj_kernel,
        grid=(b, s // tt),
        in_specs=[pl.BlockSpec((1, tt, d), lambda bi, i: (bi, i, 0)),
                  pl.BlockSpec((1, 1, d), lambda bi, i: (bi, 0, 0)),
                  pl.BlockSpec((1, 1, d), lambda bi, i: (bi, 0, 0)),
                  const((d, n)),
                  const((1, D_B)),
                  const((1, D_B)),
                  const((SGU_HEADS, CHUNK, CHUNK)),
                  const((SGU_HEADS, CHUNK, CHUNK))],
        out_specs=[pl.BlockSpec((1, tt, D_A), lambda bi, i: (bi, i, 0)),
                   pl.BlockSpec((1, tt, D_B), lambda bi, i: (bi, i, 0)),
                   pl.BlockSpec((1, tt, 2 * d), lambda bi, i: (bi, i, 0))],
        out_shape=[jax.ShapeDtypeStruct((b, s, D_A), F32),
                   jax.ShapeDtypeStruct((b, s, D_B), BF16),
                   jax.ShapeDtypeStruct((b, s, 2 * d), BF16)],
        compiler_params=_params(("arbitrary", "arbitrary")),
        name="inproj",
    )(x, shift1, scale1, w_in, lng, lnb, ws, bs_full)


def _merge_kernel(x_ref, a_ref, ap_ref, an_ref, ob_ref, sg_ref, pw_ref, ps_ref, wua_ref, wub_ref,
                  wo_ref, gate1_ref, g1_ref, b1_ref, shift2_ref, scale2_ref,
                  x1_ref, h2t_ref, ext_scr, *, seq_len):
    i = pl.program_id(1)
    ni = pl.num_programs(1)
    tt = x_ref.shape[1]
    d = x_ref.shape[2]

    ext_scr[0:HALO, :] = jnp.where(i > 0, ap_ref[0], 0.0)
    ext_scr[HALO:HALO + tt, :] = a_ref[0]
    ext_scr[HALO + tt:HALO + tt + HALO, :] = jnp.where(i < ni - 1, an_ref[0], 0.0)

    for r0 in range(0, tt, MERGE_SUB):
        rows = slice(r0, r0 + MERGE_SUB)
        pos = i * tt + r0 + lax.broadcasted_iota(jnp.int32, (MERGE_SUB, A_GROUP), 0)
        ya = jnp.zeros((MERGE_SUB, d), F32)
        for g, hw in enumerate(POOL_HALF):
            cols = slice(g * A_GROUP, (g + 1) * A_GROUP)
            base = HALO + r0
            win = ext_scr[base - hw:base - hw + MERGE_SUB, cols]
            for off in range(-hw + 1, hw):
                win = win + ext_scr[base + off:base + off + MERGE_SUB, cols]
            cnt = (jnp.minimum(pos + hw, seq_len) - jnp.maximum(pos - hw, 0)).astype(F32)
            pooled = win / cnt - ext_scr[base:base + MERGE_SUB, cols]
            mixed = jnp.dot(pooled.astype(BF16), pw_ref[g], preferred_element_type=F32) * ps_ref[:, cols]
            ya = ya + jnp.dot(mixed.astype(BF16), wua_ref[cols, :], preferred_element_type=F32)

        yb = jnp.dot(ob_ref[0, rows, :], wub_ref[...], preferred_element_type=F32)
        merged = sg_ref[0, rows, 0:d].astype(F32) * ya + sg_ref[0, rows, d:2 * d].astype(F32) * yb
        y = jnp.dot(merged.astype(BF16), wo_ref[...], preferred_element_type=F32)
        x1 = _norm0(ALPHA * x_ref[0, rows, :] + gate1_ref[0] * y) * g1_ref[...] + b1_ref[...]
        x1_ref[0, rows, :] = x1
        h2 = _norm0(x1) * (1.0 + scale2_ref[0]) + shift2_ref[0]
        h2t_ref[0, :, rows] = h2.T.astype(BF16)


def _merge(x, a, ob, sg, pool_w, pool_scale, w_up_a, w_up_b, w_out, gate1, g1, b1, shift2, scale2, tt):
    b, s, d = x.shape
    nh = tt // HALO
    last = s // HALO - 1
    vec = lambda: pl.BlockSpec((1, 1, d), lambda bi, i: (bi, 0, 0))
    row = lambda n: pl.BlockSpec((1, n), lambda bi, i: (0, 0))
    return pl.pallas_call(
        functools.partial(_merge_kernel, seq_len=s),
        grid=(b, s // tt),
        in_specs=[pl.BlockSpec((1, tt, d), lambda bi, i: (bi, i, 0)),
                  pl.BlockSpec((1, tt, D_A), lambda bi, i: (bi, i, 0)),
                  pl.BlockSpec((1, HALO, D_A), lambda bi, i: (bi, jnp.maximum(i * nh - 1, 0), 0)),
                  pl.BlockSpec((1, HALO, D_A), lambda bi, i: (bi, jnp.minimum((i + 1) * nh, last), 0)),
                  pl.BlockSpec((1, tt, D_B), lambda bi, i: (bi, i, 0)),
                  pl.BlockSpec((1, tt, 2 * d), lambda bi, i: (bi, i, 0)),
                  pl.BlockSpec((len(POOL_HALF), A_GROUP, A_GROUP), lambda bi, i: (0, 0, 0)),
                  row(D_A),
                  pl.BlockSpec((D_A, d), lambda bi, i: (0, 0), pipeline_mode=pl.Buffered(1)),
                  pl.BlockSpec((D_B, d), lambda bi, i: (0, 0), pipeline_mode=pl.Buffered(1)),
                  pl.BlockSpec((d, d), lambda bi, i: (0, 0), pipeline_mode=pl.Buffered(1)),
                  vec(), row(d), row(d), vec(), vec()],
        out_specs=[pl.BlockSpec((1, tt, d), lambda bi, i: (bi, i, 0)),
                   pl.BlockSpec((1, d, tt), lambda bi, i: (bi, 0, i))],
        out_shape=[jax.ShapeDtypeStruct((b, s, d), F32),
                   jax.ShapeDtypeStruct((b, d, s), BF16)],
        scratch_shapes=[pltpu.VMEM((tt + 2 * HALO, D_A), F32)],
        compiler_params=_params(("arbitrary", "arbitrary")),
        name="merge",
    )(x, a, a, a, ob, sg, pool_w, pool_scale, w_up_a, w_up_b, w_out, gate1, g1, b1, shift2, scale2)


def _bitonic_merge(xs, descending):
    n = len(xs)
    if n == 1:
        return xs
    h = n // 2
    big = [jnp.maximum(xs[i], xs[i + h]) for i in range(h)]
    small = [jnp.minimum(xs[i], xs[i + h]) for i in range(h)]
    first, second = (big, small) if descending else (small, big)
    return _bitonic_merge(first, descending) + _bitonic_merge(second, descending)


def _bitonic_sort(xs, descending=True):
    n = len(xs)
    if n == 1:
        return xs
    h = n // 2
    return _bitonic_merge(_bitonic_sort(xs[:h], True) + _bitonic_sort(xs[h:], False), descending)


def _merge_top(a, b):
    n = len(a)
    return _bitonic_merge([jnp.maximum(a[i], b[n - 1 - i]) for i in range(n)], True)


def _top_sorted(xs, k):
    groups = [_bitonic_sort(xs[g:g + k]) for g in range(0, len(xs), k)]
    while len(groups) > 1:
        groups = [_merge_top(groups[g], groups[g + 1]) for g in range(0, len(groups), 2)]
    return groups[0]


def _route_kernel(h2t_ref, kq_ref, a_ref, thr_ref, b_ref, s2_scr):
    tt = h2t_ref.shape[2]
    nk, nh = N_KEYS, PEER_HEADS
    half_rows = nk * nh
    neg = jnp.full((nh, LANES), NEG_INF, F32)
    per_piece = MXU_WIDTH // LANES
    piece = lambda p: jnp.dot(kq_ref[...], h2t_ref[0, :, p * MXU_WIDTH:(p + 1) * MXU_WIDTH],
                              preferred_element_type=F32)
    pieces = [piece(0)]

    for c in range(tt // LANES):
        lanes = slice(c * LANES, (c + 1) * LANES)
        if c % per_piece == 1 and len(pieces) < tt // MXU_WIDTH:
            pieces.append(piece(len(pieces)))
        s_all = pieces[c // per_piece][:, (c % per_piece) * LANES:(c % per_piece + 1) * LANES]
        s1 = [s_all[n * nh:(n + 1) * nh, :] for n in range(nk)]
        s2 = [s_all[half_rows + n * nh:half_rows + (n + 1) * nh, :] for n in range(nk)]
        v1 = _top_sorted(s1, TOPK)
        v2 = _top_sorted(s2, TOPK)

        row = lambda k1: [v1[k1] + v2[k2] for k2 in range(TOPK // (k1 + 1))]
        l0 = row(0)
        l1 = row(1) + [neg] * 8
        l2 = _bitonic_sort(row(2) + row(3) + row(4) + [neg] * 4)
        l3 = _bitonic_sort(sum([row(k1) for k1 in range(5, TOPK)], []) + [neg] * 2)
        tau = _merge_top(_merge_top(l0, l1), _merge_top(l2, l3))[TOPK - 1]

        t0 = tau - v2[0]
        e1v = [jnp.exp(v - v1[0]) for v in v1]
        e2v = [jnp.exp(v - v2[0]) for v in v2]
        thrv = [jnp.exp(t0 - v) for v in v1]
        z = jnp.zeros_like(tau)
        for (k1, k2) in _CAND:
            z = z + jnp.where(e2v[k2] >= thrv[k1], e1v[k1] * e2v[k2], 0.0)
        inv_z = 1.0 / z

        for n in range(nk):
            a_ref[0, n, :, lanes] = jnp.where(s1[n] >= v1[TOPK - 1], jnp.exp(s1[n] - v1[0]), 0.0) * inv_z
            thr_ref[0, n, :, lanes] = jnp.exp(t0 - s1[n])

        s2_scr[...] = s_all[half_rows:2 * half_rows, :]
        tau2 = v2[TOPK - 1]
        for hd in range(nh):
            s2h = s2_scr[pl.ds(hd, nk, stride=nh), :]
            b_ref[0, hd, :, lanes] = jnp.where(s2h >= tau2[hd:hd + 1, :],
                                               jnp.exp(s2h - v2[0][hd:hd + 1, :]), 0.0)


def _route(h2t, kq, tt):
    b, d, s = h2t.shape
    nk, nh = N_KEYS, PEER_HEADS
    full = lambda: pl.BlockSpec((d, d), lambda bi, i: (0, 0), pipeline_mode=pl.Buffered(1))
    eh = lambda: pl.BlockSpec((1, nk, nh, tt), lambda bi, i: (bi, 0, 0, i))
    he = lambda: pl.BlockSpec((1, nh, nk, tt), lambda bi, i: (bi, 0, 0, i))
    return pl.pallas_call(
        _route_kernel,
        grid=(b, s // tt),
        in_specs=[pl.BlockSpec((1, d, tt), lambda bi, i: (bi, 0, i)), full()],
        out_specs=[eh(), eh(), he()],
        out_shape=[jax.ShapeDtypeStruct((b, nk, nh, s), F32),
                   jax.ShapeDtypeStruct((b, nk, nh, s), F32),
                   jax.ShapeDtypeStruct((b, nh, nk, s), F32)],
        scratch_shapes=[pltpu.VMEM((nk * nh, LANES), F32)],
        compiler_params=_params(("arbitrary", "arbitrary")),
        name="route",
    )(h2t, kq)


def _experts_step(h2t_ref, u_ref, vt_ref, a_ref, thr_ref, b_ref, acc_scr, xw_scr, hu_next, hu_cur):
    te = u_ref.shape[0]
    tt = h2t_ref.shape[2]
    nsub = te // EXPERT_SUB
    for half in range(tt // EXPERT_PIECE):
        cols = slice(half * EXPERT_PIECE, (half + 1) * EXPERT_PIECE)
        for sub in range(nsub):
            rows = slice(sub * EXPERT_SUB, (sub + 1) * EXPERT_SUB)
            hu_next[rows, cols] = jnp.dot(u_ref[rows, :], h2t_ref[0, :, cols], preferred_element_type=F32)
            for l in range(EXPERT_SUB // N_KEYS):
                e1 = sub * (EXPERT_SUB // N_KEYS) + l
                for c in range(EXPERT_PIECE // LANES):
                    lo = half * EXPERT_PIECE + c * LANES
                    lanes = slice(lo, lo + LANES)
                    thr = [jnp.broadcast_to(thr_ref[0, e1, hd:hd + 1, lanes], (PACK_ROWS, LANES))
                           for hd in range(PEER_HEADS)]
                    a = [jnp.broadcast_to(a_ref[0, e1, hd:hd + 1, lanes], (PACK_ROWS, LANES))
                         for hd in range(PEER_HEADS)]
                    for r in range(N_KEYS // PACK_ROWS):
                        k2 = slice(r * PACK_ROWS, (r + 1) * PACK_ROWS)
                        w = None
                        for hd in range(PEER_HEADS):
                            bv = b_ref[0, hd, k2, lanes]
                            term = jnp.where(bv >= thr[hd], bv, 0.0) * a[hd]
                            w = term if w is None else w + term
                        k2s = slice(e1 * N_KEYS + r * PACK_ROWS, e1 * N_KEYS + (r + 1) * PACK_ROWS)
                        xw_scr[k2s, lanes] = (w * _gelu(hu_cur[k2s, lanes])).astype(BF16)
            acc_scr[:, cols] += jnp.dot(vt_ref[:, rows], xw_scr[rows, cols], preferred_element_type=F32)


def _experts_kernel(h2t_ref, u_ref, vt_ref, a_ref, thr_ref, b_ref, x1_ref, gate2_ref,
                    g2_ref, b2_ref, y_ref, acc_scr, hu_even, hu_odd, xw_scr, *, ne):
    s = pl.program_id(1)

    @pl.when(s == 0)
    def _():
        acc_scr[...] = jnp.zeros_like(acc_scr)
        hu_odd[...] = jnp.zeros_like(hu_odd)

    step = functools.partial(_experts_step, h2t_ref, u_ref, vt_ref, a_ref, thr_ref, b_ref,
                             acc_scr, xw_scr)

    @pl.when(s % 2 == 0)
    def _():
        step(hu_even, hu_odd)

    @pl.when(s % 2 == 1)
    def _():
        step(hu_odd, hu_even)

    @pl.when((s >= 1) & ((s - 1) % ne == ne - 1))
    def _():
        y2 = acc_scr[...].T
        y_ref[0] = _norm0(ALPHA * x1_ref[0] + gate2_ref[0] * y2) * g2_ref[...] + b2_ref[...]
        acc_scr[...] = jnp.zeros_like(acc_scr)


def _experts(h2t, u_bf, vt_bf, a, thr, bm, x1, gate2, g2, b2, tt, te):
    b, d, s = h2t.shape
    nk, nh = N_KEYS, PEER_HEADS
    nl = te // nk
    ni = s // tt
    ne = N_EXPERTS // te
    nxt_i = lambda st: jnp.minimum(st // ne, ni - 1)
    cur = lambda st: jnp.maximum(st - 1, 0)
    eh = lambda: pl.BlockSpec((1, nl, nh, tt), lambda bi, st: (bi, cur(st) % ne, 0, cur(st) // ne))
    he = lambda: pl.BlockSpec((1, nh, nk, tt), lambda bi, st: (bi, 0, 0, cur(st) // ne))
    row = lambda: pl.BlockSpec((1, d), lambda bi, st: (0, 0))
    return pl.pallas_call(
        functools.partial(_experts_kernel, ne=ne),
        grid=(b, ni * ne + 1),
        in_specs=[pl.BlockSpec((1, d, tt), lambda bi, st: (bi, 0, nxt_i(st))),
                  pl.BlockSpec((te, d), lambda bi, st: (st % ne, 0)),
                  pl.BlockSpec((d, te), lambda bi, st: (0, cur(st) % ne)),
                  eh(), eh(), he(),
                  pl.BlockSpec((1, tt, d), lambda bi, st: (bi, cur(st) // ne, 0)),
                  pl.BlockSpec((1, 1, d), lambda bi, st: (bi, 0, 0)),
                  row(), row()],
        out_specs=pl.BlockSpec((1, tt, d), lambda bi, st: (bi, cur(st) // ne, 0)),
        out_shape=jax.ShapeDtypeStruct((b, s, d), F32),
        scratch_shapes=[pltpu.VMEM((d, tt), F32),
                        pltpu.VMEM((te, tt), F32),
                        pltpu.VMEM((te, tt), F32),
                        pltpu.VMEM((te, tt), BF16)],
        compiler_params=_params(("arbitrary", "arbitrary")),
        name="experts",
    )(h2t, u_bf, vt_bf, a, thr, bm, x1, gate2, g2, b2)


def _block_diag_keys(keys):
    nh, _, nk, c = keys.shape
    eye = jnp.eye(nh, dtype=keys.dtype)
    full = jnp.einsum("hknc,hg,kj->knhgjc", keys, eye, jnp.eye(2, dtype=keys.dtype))
    return full.reshape(2 * nk * nh, nh * 2 * c)


def _trunk(x, mod, wts, tiles):
    b, s, d = x.shape
    shift1, scale1, gate1, shift2, scale2, gate2 = [mod[:, k][:, None, :] for k in range(6)]
    t_in, t_merge, t_route, t_exp, t_e = tiles
    a, ob, sg = _inproj(x, shift1, scale1, wts["w_in"], wts["sgu_ln_g"], wts["sgu_ln_b"],
                        wts["sgu_w"], wts["sgu_b_full"], min(t_in, s))
    x1, h2t = _merge(x, a, ob, sg, wts["pool_w"], wts["pool_scale"], wts["w_up_a"], wts["w_up_b"],
                     wts["w_out"], gate1, wts["ln1_g"], wts["ln1_b"], shift2, scale2, min(t_merge, s))
    ra, rthr, rb = _route(h2t, wts["kq"], min(t_route, s))
    return _experts(h2t, wts["u_bf"], wts["vt_bf"], ra, rthr, rb, x1, gate2,
                    wts["ln2_g"], wts["ln2_b"], min(t_exp, s), t_e)


def _prepare(w_ada, b_ada, w_in, pool_w, pool_scale, sgu_ln_g, sgu_ln_b, sgu_w, sgu_b, w_up_a, w_up_b,
             w_out, ln1_g, ln1_b, peer_wq, peer_keys, peer_u, peer_v, ln2_g, ln2_b):
    l = 0
    return {
        "w_in": w_in[l].astype(BF16),
        "pool_w": pool_w[l].astype(BF16),
        "pool_scale": pool_scale[l][None, :],
        "sgu_ln_g": sgu_ln_g[l][None, :],
        "sgu_ln_b": sgu_ln_b[l][None, :],
        "sgu_w": sgu_w[l].astype(BF16),
        "sgu_b_full": jnp.broadcast_to(sgu_b[l][:, :, None], (SGU_HEADS, CHUNK, CHUNK)),
        "w_up_a": w_up_a[l].astype(BF16),
        "w_up_b": w_up_b[l].astype(BF16),
        "w_out": w_out[l].astype(BF16),
        "ln1_g": ln1_g[l][None, :],
        "ln1_b": ln1_b[l][None, :],
        "kq": _fold(_block_diag_keys(peer_keys[l]).astype(BF16), _transpose_cast(peer_wq[l])),
        "u_bf": peer_u[l].astype(BF16),
        "vt_bf": _transpose_cast(peer_v[l]),
        "ln2_g": ln2_g[l][None, :],
        "ln2_b": ln2_b[l][None, :],
    }


_TILES = (512, 512, 512, 512, 1024)


def kernel(x_prompt, x_sample, c_prompt, c_sample, w_ada, b_ada, w_in, pool_w, pool_scale, sgu_ln_g,
           sgu_ln_b, sgu_w, sgu_b, w_up_a, w_up_b, w_out, ln1_g, ln1_b, peer_wq, peer_keys, peer_u,
           peer_v, ln2_g, ln2_b):
    assert w_ada.shape[0] == DEPTH == 1
    bp = x_prompt.shape[0]
    bs = x_sample.shape[0]
    c_all = jnp.concatenate([c_prompt, c_sample], axis=0)
    c8 = jnp.pad(c_all, ((0, 8 - bp - bs), (0, 0)))
    mod = _ada(c8, w_ada[0], b_ada[0][None, :]).reshape(8, 6, D_MODEL)
    wts = _prepare(w_ada, b_ada, w_in, pool_w, pool_scale, sgu_ln_g, sgu_ln_b, sgu_w, sgu_b, w_up_a,
                   w_up_b, w_out, ln1_g, ln1_b, peer_wq, peer_keys, peer_u, peer_v, ln2_g, ln2_b)
    y_prompt = _trunk(x_prompt, mod[:bp], wts, _TILES)
    y_sample = _trunk(x_sample, mod[bp:bp + bs], wts, _TILES)
    return (y_prompt, y_sample)
```

```python
import functools
import math

import jax
import jax.numpy as jnp
from jax import lax
from jax.experimental import pallas as pl
from jax.experimental.pallas import tpu as pltpu

F32 = jnp.float32
BF16 = jnp.bfloat16

D_MODEL = 2048
D_A = 1024
D_B = 1024
POOL_HALF = (1, 2, 4, 8)
A_GROUP = 256
CHUNK = 128
SGU_HEADS = 8
PEER_HEADS = 8
N_KEYS = 128
N_EXPERTS = N_KEYS * N_KEYS
TOPK = 16
EXPERT_SUB = 256
DEPTH = 1
ALPHA = (2.0 * DEPTH) ** 0.25
LN_EPS = 1e-5
HALO = 8
LANES = 128
MXU_WIDTH = 256
EXPERT_PIECE = MXU_WIDTH
INPROJ_TILE = 1024
PREP_TILE = 512
MERGE_SUB = 256
INPROJ_SUB = 256
ADA_TILE = 1024
PACK_ROWS = 16
NEG_INF = float("-inf")
VMEM_LIMIT = 60 * 1024 * 1024

_CAND = tuple((k1, k2) for k1 in range(TOPK) for k2 in range(TOPK) if (k1 + 1) * (k2 + 1) <= TOPK)


def _gelu(x):
    c = math.sqrt(2.0 / math.pi)
    return x * (0.5 * (1.0 + jnp.tanh(c * (x + 0.044715 * (x * x * x)))))


def _sigmoid(x):
    return 1.0 / (1.0 + jnp.exp(-x))


def _norm0(x):
    mu = jnp.mean(x, axis=-1, keepdims=True)
    xc = x - mu
    var = jnp.mean(xc * xc, axis=-1, keepdims=True)
    return xc * lax.rsqrt(var + LN_EPS)


def _params(sem):
    return pltpu.CompilerParams(dimension_semantics=sem, vmem_limit_bytes=VMEM_LIMIT)


def _ada_kernel(c_ref, w_ref, b_ref, o_ref):
    c = c_ref[...]
    sc = c * _sigmoid(c)
    o_ref[...] = jnp.dot(sc.astype(BF16), w_ref[...].astype(BF16),
                         preferred_element_type=F32) + b_ref[...]


def _ada(c8, w_ada, b_ada):
    n = w_ada.shape[1]
    tn = ADA_TILE
    return pl.pallas_call(
        _ada_kernel,
        grid=(n // tn,),
        in_specs=[pl.BlockSpec((8, D_MODEL), lambda j: (0, 0)),
                  pl.BlockSpec((D_MODEL, tn), lambda j: (0, j)),
                  pl.BlockSpec((1, tn), lambda j: (0, j))],
        out_specs=pl.BlockSpec((8, tn), lambda j: (0, j)),
        out_shape=jax.ShapeDtypeStruct((8, n), F32),
        compiler_params=_params(("arbitrary",)),
        name="ada",
    )(c8, w_ada, b_ada)


def _tcast_kernel(x_ref, o_ref):
    o_ref[...] = x_ref[...].T.astype(BF16)


def _transpose_cast(x):
    r, c = x.shape
    tr = PREP_TILE
    return pl.pallas_call(
        _tcast_kernel,
        grid=(r // tr,),
        in_specs=[pl.BlockSpec((tr, c), lambda i: (i, 0))],
        out_specs=pl.BlockSpec((c, tr), lambda i: (0, i)),
        out_shape=jax.ShapeDtypeStruct((c, r), BF16),
        compiler_params=_params(("arbitrary",)),
        name="transpose_cast",
    )(x)


def _fold_kernel(a_ref, b_ref, o_ref):
    o_ref[...] = jnp.dot(a_ref[...], b_ref[...], preferred_element_type=F32).astype(BF16)


def _fold(a, b):
    m, k = a.shape
    n = b.shape[1]
    tn = PREP_TILE
    return pl.pallas_call(
        _fold_kernel,
        grid=(n // tn,),
        in_specs=[pl.BlockSpec((m, k), lambda j: (0, 0)),
                  pl.BlockSpec((k, tn), lambda j: (0, j))],
        out_specs=pl.BlockSpec((m, tn), lambda j: (0, j)),
        out_shape=jax.ShapeDtypeStruct((m, n), BF16),
        compiler_params=_params(("arbitrary",)),
        name="fold_keys",
    )(a, b)


def _inproj_kernel(x_ref, shift_ref, scale_ref, w_ref, lng_ref, lnb_ref, ws_ref, bs_ref,
                   a_ref, ob_ref, sg_ref):
    tt = x_ref.shape[1]
    d = x_ref.shape[2]
    for r0 in range(0, tt, INPROJ_SUB):
        rsub = slice(r0, r0 + INPROJ_SUB)
        h = (_norm0(x_ref[0, rsub, :]) * (1.0 + scale_ref[0]) + shift_ref[0]).astype(BF16)
        proj = lambda lo, hi: jnp.dot(h, w_ref[:, lo:hi], preferred_element_type=F32)
        a_ref[0, rsub, :] = proj(0, D_A)
        gu = _gelu(proj(D_A, D_A + D_B)).astype(BF16)
        vn = (_norm0(_gelu(proj(D_A + D_B, D_A + 2 * D_B))) * lng_ref[...] + lnb_ref[...]).astype(BF16)
        for c in range(INPROJ_SUB // CHUNK):
            rows = slice(c * CHUNK, (c + 1) * CHUNK)
            orow = slice(r0 + c * CHUNK, r0 + (c + 1) * CHUNK)
            for hd in range(SGU_HEADS):
                cols = slice(hd * CHUNK, (hd + 1) * CHUNK)
                mixed = jnp.dot(ws_ref[hd], vn[rows, cols], preferred_element_type=F32) + bs_ref[hd]
                ob_ref[0, orow, cols] = (gu[rows, cols].astype(F32) * mixed).astype(BF16)
        g0 = D_A + 2 * D_B
        for t in range(2 * d // INPROJ_TILE):
            cols = slice(t * INPROJ_TILE, (t + 1) * INPROJ_TILE)
            sg_ref[0, rsub, cols] = _sigmoid(proj(g0 + t * INPROJ_TILE, g0 + (t + 1) * INPROJ_TILE)).astype(BF16)


def _inproj(x, shift1, scale1, w_in, lng, lnb, ws, bs_full, tt):
    b, s, d = x.shape
    n = w_in.shape[1]
    const = lambda shape: pl.BlockSpec(shape, lambda bi, i: (0,) * len(shape), pipeline_mode=pl.Buffered(1))
    return pl.pallas_call(
        _inproj_kernel,
        grid=(b, s // tt),
        in_specs=[pl.BlockSpec((1, tt, d), lambda bi, i: (bi, i, 0)),
                  pl.BlockSpec((1, 1, d), lambda bi, i: (bi, 0, 0)),
                  pl.BlockSpec((1, 1, d), lambda bi, i: (bi, 0, 0)),
                  const((d, n)),
                  const((1, D_B)),
                  const((1, D_B)),
                  const((SGU_HEADS, CHUNK, CHUNK)),
                  const((SGU_HEADS, CHUNK, CHUNK))],
        out_specs=[pl.BlockSpec((1, tt, D_A), lambda bi, i: (bi, i, 0)),
                   pl.BlockSpec((1, tt, D_B), lambda bi, i: (bi, i, 0)),
                   pl.BlockSpec((1, tt, 2 * d), lambda bi, i: (bi, i, 0))],
        out_shape=[jax.ShapeDtypeStruct((b, s, D_A), F32),
                   jax.ShapeDtypeStruct((b, s, D_B), BF16),
                   jax.ShapeDtypeStruct((b, s, 2 * d), BF16)],
        compiler_params=_params(("arbitrary", "arbitrary")),
        name="inproj",
    )(x, shift1, scale1, w_in, lng, lnb, ws, bs_full)


def _merge_kernel(x_ref, a_ref, ap_ref, an_ref, ob_ref, sg_ref, pw_ref, ps_ref, wua_ref, wub_ref,
                  wo_ref, gate1_ref, g1_ref, b1_ref, shift2_ref, scale2_ref,
                  x1_ref, h2t_ref, ext_scr, *, seq_len):
    i = pl.program_id(1)
    ni = pl.num_programs(1)
    tt = x_ref.shape[1]
    d = x_ref.shape[2]

    ext_scr[0:HALO, :] = jnp.where(i > 0, ap_ref[0], 0.0)
    ext_scr[HALO:HALO + tt, :] = a_ref[0]
    ext_scr[HALO + tt:HALO + tt + HALO, :] = jnp.where(i < ni - 1, an_ref[0], 0.0)

    for r0 in range(0, tt, MERGE_SUB):
        rows = slice(r0, r0 + MERGE_SUB)
        pos = i * tt + r0 + lax.broadcasted_iota(jnp.int32, (MERGE_SUB, A_GROUP), 0)
        ya = jnp.zeros((MERGE_SUB, d), F32)
        for g, hw in enumerate(POOL_HALF):
            cols = slice(g * A_GROUP, (g + 1) * A_GROUP)
            base = HALO + r0
            win = ext_scr[base - hw:base - hw + MERGE_SUB, cols]
            for off in range(-hw + 1, hw):
                win = win + ext_scr[base + off:base + off + MERGE_SUB, cols]
            cnt = (jnp.minimum(pos + hw, seq_len) - jnp.maximum(pos - hw, 0)).astype(F32)
            pooled = win / cnt - ext_scr[base:base + MERGE_SUB, cols]
            mixed = jnp.dot(pooled.astype(BF16), pw_ref[g], preferred_element_type=F32) * ps_ref[:, cols]
            ya = ya + jnp.dot(mixed.astype(BF16), wua_ref[cols, :], preferred_element_type=F32)

        yb = jnp.dot(ob_ref[0, rows, :], wub_ref[...], preferred_element_type=F32)
        merged = sg_ref[0, rows, 0:d].astype(F32) * ya + sg_ref[0, rows, d:2 * d].astype(F32) * yb
        y = jnp.dot(merged.astype(BF16), wo_ref[...], preferred_element_type=F32)
        x1 = _norm0(ALPHA * x_ref[0, rows, :] + gate1_ref[0] * y) * g1_ref[...] + b1_ref[...]
        x1_ref[0, rows, :] = x1
        h2 = _norm0(x1) * (1.0 + scale2_ref[0]) + shift2_ref[0]
        h2t_ref[0, :, rows] = h2.T.astype(BF16)


def _merge(x, a, ob, sg, pool_w, pool_scale, w_up_a, w_up_b, w_out, gate1, g1, b1, shift2, scale2, tt):
    b, s, d = x.shape
    nh = tt // HALO
    last = s // HALO - 1
    vec = lambda: pl.BlockSpec((1, 1, d), lambda bi, i: (bi, 0, 0))
    row = lambda n: pl.BlockSpec((1, n), lambda bi, i: (0, 0))
    return pl.pallas_call(
        functools.partial(_merge_kernel, seq_len=s),
        grid=(b, s // tt),
        in_specs=[pl.BlockSpec((1, tt, d), lambda bi, i: (bi, i, 0)),
                  pl.BlockSpec((1, tt, D_A), lambda bi, i: (bi, i, 0)),
                  pl.BlockSpec((1, HALO, D_A), lambda bi, i: (bi, jnp.maximum(i * nh - 1, 0), 0)),
                  pl.BlockSpec((1, HALO, D_A), lambda bi, i: (bi, jnp.minimum((i + 1) * nh, last), 0)),
                  pl.BlockSpec((1, tt, D_B), lambda bi, i: (bi, i, 0)),
                  pl.BlockSpec((1, tt, 2 * d), lambda bi, i: (bi, i, 0)),
                  pl.BlockSpec((len(POOL_HALF), A_GROUP, A_GROUP), lambda bi, i: (0, 0, 0)),
                  row(D_A),
                  pl.BlockSpec((D_A, d), lambda bi, i: (0, 0), pipeline_mode=pl.Buffered(1)),
                  pl.BlockSpec((D_B, d), lambda bi, i: (0, 0), pipeline_mode=pl.Buffered(1)),
                  pl.BlockSpec((d, d), lambda bi, i: (0, 0), pipeline_mode=pl.Buffered(1)),
                  vec(), row(d), row(d), vec(), vec()],
        out_specs=[pl.BlockSpec((1, tt, d), lambda bi, i: (bi, i, 0)),
                   pl.BlockSpec((1, d, tt), lambda bi, i: (bi, 0, i))],
        out_shape=[jax.ShapeDtypeStruct((b, s, d), F32),
                   jax.ShapeDtypeStruct((b, d, s), BF16)],
        scratch_shapes=[pltpu.VMEM((tt + 2 * HALO, D_A), F32)],
        compiler_params=_params(("arbitrary", "arbitrary")),
        name="merge",
    )(x, a, a, a, ob, sg, pool_w, pool_scale, w_up_a, w_up_b, w_out, gate1, g1, b1, shift2, scale2)


def _bitonic_merge(xs, descending):
    n = len(xs)
    if n == 1:
        return xs
    h = n // 2
    big = [jnp.maximum(xs[i], xs[i + h]) for i in range(h)]
    small = [jnp.minimum(xs[i], xs[i + h]) for i in range(h)]
    first, second = (big, small) if descending else (small, big)
    return _bitonic_merge(first, descending) + _bitonic_merge(second, descending)


def _bitonic_sort(xs, descending=True):
    n = len(xs)
    if n == 1:
        return xs
    h = n // 2
    return _bitonic_merge(_bitonic_sort(xs[:h], True) + _bitonic_sort(xs[h:], False), descending)


def _merge_top(a, b):
    n = len(a)
    return _bitonic_merge([jnp.maximum(a[i], b[n - 1 - i]) for i in range(n)], True)


def _top_sorted(xs, k):
    groups = [_bitonic_sort(xs[g:g + k]) for g in range(0, len(xs), k)]
    while len(groups) > 1:
        groups = [_merge_top(groups[g], groups[g + 1]) for g in range(0, len(groups), 2)]
    return groups[0]


def _route_kernel(h2t_ref, kq_ref, a_ref, thr_ref, b_ref, s2_scr):
    tt = h2t_ref.shape[2]
    nk, nh = N_KEYS, PEER_HEADS
    half_rows = nk * nh
    neg = jnp.full((nh, LANES), NEG_INF, F32)
    per_piece = MXU_WIDTH // LANES
    piece = lambda p: jnp.dot(kq_ref[...], h2t_ref[0, :, p * MXU_WIDTH:(p + 1) * MXU_WIDTH],
                              preferred_element_type=F32)
    pieces = [piece(0)]

    for c in range(tt // LANES):
        lanes = slice(c * LANES, (c + 1) * LANES)
        if c % per_piece == 1 and len(pieces) < tt // MXU_WIDTH:
            pieces.append(piece(len(pieces)))
        s_all = pieces[c // per_piece][:, (c % per_piece) * LANES:(c % per_piece + 1) * LANES]
        s1 = [s_all[n * nh:(n + 1) * nh, :] for n in range(nk)]
        s2 = [s_all[half_rows + n * nh:half_rows + (n + 1) * nh, :] for n in range(nk)]
        v1 = _top_sorted(s1, TOPK)
        v2 = _top_sorted(s2, TOPK)

        row = lambda k1: [v1[k1] + v2[k2] for k2 in range(TOPK // (k1 + 1))]
        l0 = row(0)
        l1 = row(1) + [neg] * 8
        l2 = _bitonic_sort(row(2) + row(3) + row(4) + [neg] * 4)
        l3 = _bitonic_sort(sum([row(k1) for k1 in range(5, TOPK)], []) + [neg] * 2)
        tau = _merge_top(_merge_top(l0, l1), _merge_top(l2, l3))[TOPK - 1]

        t0 = tau - v2[0]
        e1v = [jnp.exp(v - v1[0]) for v in v1]
        e2v = [jnp.exp(v - v2[0]) for v in v2]
        thrv = [jnp.exp(t0 - v) for v in v1]
        z = jnp.zeros_like(tau)
        for (k1, k2) in _CAND:
            z = z + jnp.where(e2v[k2] >= thrv[k1], e1v[k1] * e2v[k2], 0.0)
        inv_z = 1.0 / z

        for n in range(nk):
            a_ref[0, n, :, lanes] = jnp.where(s1[n] >= v1[TOPK - 1], jnp.exp(s1[n] - v1[0]), 0.0) * inv_z
            thr_ref[0, n, :, lanes] = jnp.exp(t0 - s1[n])

        s2_scr[...] = s_all[half_rows:2 * half_rows, :]
        tau2 = v2[TOPK - 1]
        for hd in range(nh):
            s2h = s2_scr[pl.ds(hd, nk, stride=nh), :]
            b_ref[0, hd, :, lanes] = jnp.where(s2h >= tau2[hd:hd + 1, :],
                                               jnp.exp(s2h - v2[0][hd:hd + 1, :]), 0.0)


def _route(h2t, kq, tt):
    b, d, s = h2t.shape
    nk, nh = N_KEYS, PEER_HEADS
    full = lambda: pl.BlockSpec((d, d), lambda bi, i: (0, 0), pipeline_mode=pl.Buffered(1))
    eh = lambda: pl.BlockSpec((1, nk, nh, tt), lambda bi, i: (bi, 0, 0, i))
    he = lambda: pl.BlockSpec((1, nh, nk, tt), lambda bi, i: (bi, 0, 0, i))
    return pl.pallas_call(
        _route_kernel,
        grid=(b, s // tt),
        in_specs=[pl.BlockSpec((1, d, tt), lambda bi, i: (bi, 0, i)), full()],
        out_specs=[eh(), eh(), he()],
        out_shape=[jax.ShapeDtypeStruct((b, nk, nh, s), F32),
                   jax.ShapeDtypeStruct((b, nk, nh, s), F32),
                   jax.ShapeDtypeStruct((b, nh, nk, s), F32)],
        scratch_shapes=[pltpu.VMEM((nk * nh, LANES), F32)],
        compiler_params=_params(("arbitrary", "arbitrary")),
        name="route",
    )(h2t, kq)


def _experts_step(h2t_ref, u_ref, vt_ref, a_ref, thr_ref, b_ref, acc_scr, xw_scr, hu_next, hu_cur):
    te = u_ref.shape[0]
    tt = h2t_ref.shape[2]
    nsub = te // EXPERT_SUB
    for half in range(tt // EXPERT_PIECE):
        cols = slice(half * EXPERT_PIECE, (half + 1) * EXPERT_PIECE)
        for sub in range(nsub):
            rows = slice(sub * EXPERT_SUB, (sub + 1) * EXPERT_SUB)
            hu_next[rows, cols] = jnp.dot(u_ref[rows, :], h2t_ref[0, :, cols], preferred_element_type=F32)
            for l in range(EXPERT_SUB // N_KEYS):
                e1 = sub * (EXPERT_SUB // N_KEYS) + l
                for c in range(EXPERT_PIECE // LANES):
                    lo = half * EXPERT_PIECE + c * LANES
                    lanes = slice(lo, lo + LANES)
                    thr = [jnp.broadcast_to(thr_ref[0, e1, hd:hd + 1, lanes], (PACK_ROWS, LANES))
                           for hd in range(PEER_HEADS)]
                    a = [jnp.broadcast_to(a_ref[0, e1, hd:hd + 1, lanes], (PACK_ROWS, LANES))
                         for hd in range(PEER_HEADS)]
                    for r in range(N_KEYS // PACK_ROWS):
                        k2 = slice(r * PACK_ROWS, (r + 1) * PACK_ROWS)
                        w = None
                        for hd in range(PEER_HEADS):
                            bv = b_ref[0, hd, k2, lanes]
                            term = jnp.where(bv >= thr[hd], bv, 0.0) * a[hd]
                            w = term if w is None else w + term
                        k2s = slice(e1 * N_KEYS + r * PACK_ROWS, e1 * N_KEYS + (r + 1) * PACK_ROWS)
                        xw_scr[k2s, lanes] = (w * _gelu(hu_cur[k2s, lanes])).astype(BF16)
            acc_scr[:, cols] += jnp.dot(vt_ref[:, rows], xw_scr[rows, cols], preferred_element_type=F32)


def _experts_kernel(h2t_ref, u_ref, vt_ref, a_ref, thr_ref, b_ref, x1_ref, gate2_ref,
                    g2_ref, b2_ref, y_ref, acc_scr, hu_even, hu_odd, xw_scr, *, ne):
    s = pl.program_id(1)

    @pl.when(s == 0)
    def _():
        acc_scr[...] = jnp.zeros_like(acc_scr)
        hu_odd[...] = jnp.zeros_like(hu_odd)

    step = functools.partial(_experts_step, h2t_ref, u_ref, vt_ref, a_ref, thr_ref, b_ref,
                             acc_scr, xw_scr)

    @pl.when(s % 2 == 0)
    def _():
        step(hu_even, hu_odd)

    @pl.when(s % 2 == 1)
    def _():
        step(hu_odd, hu_even)

    @pl.when((s >= 1) & ((s - 1) % ne == ne - 1))
    def _():
        y2 = acc_scr[...].T
        y_ref[0] = _norm0(ALPHA * x1_ref[0] + gate2_ref[0] * y2) * g2_ref[...] + b2_ref[...]
        acc_scr[...] = jnp.zeros_like(acc_scr)


def _experts(h2t, u_bf, vt_bf, a, thr, bm, x1, gate2, g2, b2, tt, te):
    b, d, s = h2t.shape
    nk, nh = N_KEYS, PEER_HEADS
    nl = te // nk
    ni = s // tt
    ne = N_EXPERTS // te
    nxt_i = lambda st: jnp.minimum(st // ne, ni - 1)
    cur = lambda st: jnp.maximum(st - 1, 0)
    eh = lambda: pl.BlockSpec((1, nl, nh, tt), lambda bi, st: (bi, cur(st) % ne, 0, cur(st) // ne))
    he = lambda: pl.BlockSpec((1, nh, nk, tt), lambda bi, st: (bi, 0, 0, cur(st) // ne))
    row = lambda: pl.BlockSpec((1, d), lambda bi, st: (0, 0))
    return pl.pallas_call(
        functools.partial(_experts_kernel, ne=ne),
        grid=(b, ni * ne + 1),
        in_specs=[pl.BlockSpec((1, d, tt), lambda bi, st: (bi, 0, nxt_i(st))),
                  pl.BlockSpec((te, d), lambda bi, st: (st % ne, 0)),
                  pl.BlockSpec((d, te), lambda bi, st: (0, cur(st) % ne)),
                  eh(), eh(), he(),
                  pl.BlockSpec((1, tt, d), lambda bi, st: (bi, cur(st) // ne, 0)),
                  pl.BlockSpec((1, 1, d), lambda bi, st: (bi, 0, 0)),
                  row(), row()],
        out_specs=pl.BlockSpec((1, tt, d), lambda bi, st: (bi, cur(st) // ne, 0)),
        out_shape=jax.ShapeDtypeStruct((b, s, d), F32),
        scratch_shapes=[pltpu.VMEM((d, tt), F32),
                        pltpu.VMEM((te, tt), F32),
                        pltpu.VMEM((te, tt), F32),
                        pltpu.VMEM((te, tt), BF16)],
        compiler_params=_params(("arbitrary", "arbitrary")),
        name="experts",
    )(h2t, u_bf, vt_bf, a, thr, bm, x1, gate2, g2, b2)


def _block_diag_keys(keys):
    nh, _, nk, c = keys.shape
    eye = jnp.eye(nh, dtype=keys.dtype)
    full = jnp.einsum("hknc,hg,kj->knhgjc", keys, eye, jnp.eye(2, dtype=keys.dtype))
    return full.reshape(2 * nk * nh, nh * 2 * c)


def _trunk(x, mod, wts, tiles):
    b, s, d = x.shape
    shift1, scale1, gate1, shift2, scale2, gate2 = [mod[:, k][:, None, :] for k in range(6)]
    t_in, t_merge, t_route, t_exp, t_e = tiles
    a, ob, sg = _inproj(x, shift1, scale1, wts["w_in"], wts["sgu_ln_g"], wts["sgu_ln_b"],
                        wts["sgu_w"], wts["sgu_b_full"], min(t_in, s))
    x1, h2t = _merge(x, a, ob, sg, wts["pool_w"], wts["pool_scale"], wts["w_up_a"], wts["w_up_b"],
                     wts["w_out"], gate1, wts["ln1_g"], wts["ln1_b"], shift2, scale2, min(t_merge, s))
    ra, rthr, rb = _route(h2t, wts["kq"], min(t_route, s))
    return _experts(h2t, wts["u_bf"], wts["vt_bf"], ra, rthr, rb, x1, gate2,
                    wts["ln2_g"], wts["ln2_b"], min(t_exp, s), t_e)


def _prepare(w_ada, b_ada, w_in, pool_w, pool_scale, sgu_ln_g, sgu_ln_b, sgu_w, sgu_b, w_up_a, w_up_b,
             w_out, ln1_g, ln1_b, peer_wq, peer_keys, peer_u, peer_v, ln2_g, ln2_b):
    l = 0
    return {
        "w_in": w_in[l].astype(BF16),
        "pool_w": pool_w[l].astype(BF16),
        "pool_scale": pool_scale[l][None, :],
        "sgu_ln_g": sgu_ln_g[l][None, :],
        "sgu_ln_b": sgu_ln_b[l][None, :],
        "sgu_w": sgu_w[l].astype(BF16),
        "sgu_b_full": jnp.broadcast_to(sgu_b[l][:, :, None], (SGU_HEADS, CHUNK, CHUNK)),
        "w_up_a": w_up_a[l].astype(BF16),
        "w_up_b": w_up_b[l].astype(BF16),
        "w_out": w_out[l].astype(BF16),
        "ln1_g": ln1_g[l][None, :],
        "ln1_b": ln1_b[l][None, :],
        "kq": _fold(_block_diag_keys(peer_keys[l]).astype(BF16), _transpose_cast(peer_wq[l])),
        "u_bf": peer_u[l].astype(BF16),
        "vt_bf": _transpose_cast(peer_v[l]),
        "ln2_g": ln2_g[l][None, :],
        "ln2_b": ln2_b[l][None, :],
    }


_TILES = (512, 512, 1024, 512, 1024)


def kernel(x_prompt, x_sample, c_prompt, c_sample, w_ada, b_ada, w_in, pool_w, pool_scale, sgu_ln_g,
           sgu_ln_b, sgu_w, sgu_b, w_up_a, w_up_b, w_out, ln1_g, ln1_b, peer_wq, peer_keys, peer_u,
           peer_v, ln2_g, ln2_b):
    assert w_ada.shape[0] == DEPTH == 1
    bp = x_prompt.shape[0]
    bs = x_sample.shape[0]
    c_all = jnp.concatenate([c_prompt, c_sample], axis=0)
    c8 = jnp.pad(c_all, ((0, 8 - bp - bs), (0, 0)))
    mod = _ada(c8, w_ada[0], b_ada[0][None, :]).reshape(8, 6, D_MODEL)
    wts = _prepare(w_ada, b_ada, w_in, pool_w, pool_scale, sgu_ln_g, sgu_ln_b, sgu_w, sgu_b, w_up_a,
                   w_up_b, w_out, ln1_g, ln1_b, peer_wq, peer_keys, peer_u, peer_v, ln2_g, ln2_b)
    y_prompt = _trunk(x_prompt, mod[:bp], wts, _TILES)
    y_sample = _trunk(x_sample, mod[bp:bp + bs], wts, _TILES)
    return (y_prompt, y_sample)
```

```python
import functools
import math

import jax
import jax.numpy as jnp
from jax import lax
from jax.experimental import pallas as pl
from jax.experimental.pallas import tpu as pltpu

F32 = jnp.float32
BF16 = jnp.bfloat16

D_MODEL = 2048
D_A = 1024
D_B = 1024
POOL_HALF = (1, 2, 4, 8)
A_GROUP = 256
CHUNK = 128
SGU_HEADS = 8
PEER_HEADS = 8
N_KEYS = 128
N_EXPERTS = N_KEYS * N_KEYS
TOPK = 16
EXPERT_SUB = 256
DEPTH = 1
ALPHA = (2.0 * DEPTH) ** 0.25
LN_EPS = 1e-5
HALO = 8
LANES = 128
MXU_WIDTH = 256
EXPERT_PIECE = MXU_WIDTH
INPROJ_TILE = 1024
PREP_TILE = 512
MERGE_SUB = 256
INPROJ_SUB = 256
ADA_TILE = 1024
PACK_ROWS = 16
NEG_INF = float("-inf")
VMEM_LIMIT = 60 * 1024 * 1024

_CAND = tuple((k1, k2) for k1 in range(TOPK) for k2 in range(TOPK) if (k1 + 1) * (k2 + 1) <= TOPK)


def _gelu(x):
    c = math.sqrt(2.0 / math.pi)
    return x * (0.5 * (1.0 + jnp.tanh(c * (x + 0.044715 * (x * x * x)))))


def _sigmoid(x):
    return 1.0 / (1.0 + jnp.exp(-x))


def _norm0(x):
    mu = jnp.mean(x, axis=-1, keepdims=True)
    xc = x - mu
    var = jnp.mean(xc * xc, axis=-1, keepdims=True)
    return xc * lax.rsqrt(var + LN_EPS)


def _params(sem):
    return pltpu.CompilerParams(dimension_semantics=sem, vmem_limit_bytes=VMEM_LIMIT)


def _ada_kernel(c_ref, w_ref, b_ref, o_ref):
    c = c_ref[...]
    sc = c * _sigmoid(c)
    o_ref[...] = jnp.dot(sc.astype(BF16), w_ref[...].astype(BF16),
                         preferred_element_type=F32) + b_ref[...]


def _ada(c8, w_ada, b_ada):
    n = w_ada.shape[1]
    tn = ADA_TILE
    return pl.pallas_call(
        _ada_kernel,
        grid=(n // tn,),
        in_specs=[pl.BlockSpec((8, D_MODEL), lambda j: (0, 0)),
                  pl.BlockSpec((D_MODEL, tn), lambda j: (0, j)),
                  pl.BlockSpec((1, tn), lambda j: (0, j))],
        out_specs=pl.BlockSpec((8, tn), lambda j: (0, j)),
        out_shape=jax.ShapeDtypeStruct((8, n), F32),
        compiler_params=_params(("arbitrary",)),
        name="ada",
    )(c8, w_ada, b_ada)


def _tcast_kernel(x_ref, o_ref):
    o_ref[...] = x_ref[...].T.astype(BF16)


def _transpose_cast(x):
    r, c = x.shape
    tr = PREP_TILE
    return pl.pallas_call(
        _tcast_kernel,
        grid=(r // tr,),
        in_specs=[pl.BlockSpec((tr, c), lambda i: (i, 0))],
        out_specs=pl.BlockSpec((c, tr), lambda i: (0, i)),
        out_shape=jax.ShapeDtypeStruct((c, r), BF16),
        compiler_params=_params(("arbitrary",)),
        name="transpose_cast",
    )(x)


def _fold_kernel(a_ref, b_ref, o_ref):
    o_ref[...] = jnp.dot(a_ref[...], b_ref[...], preferred_element_type=F32).astype(BF16)


def _fold(a, b):
    m, k = a.shape
    n = b.shape[1]
    tn = PREP_TILE
    return pl.pallas_call(
        _fold_kernel,
        grid=(n // tn,),
        in_specs=[pl.BlockSpec((m, k), lambda j: (0, 0)),
                  pl.BlockSpec((k, tn), lambda j: (0, j))],
        out_specs=pl.BlockSpec((m, tn), lambda j: (0, j)),
        out_shape=jax.ShapeDtypeStruct((m, n), BF16),
        compiler_params=_params(("arbitrary",)),
        name="fold_keys",
    )(a, b)


def _inproj_kernel(x_ref, shift_ref, scale_ref, w_ref, lng_ref, lnb_ref, ws_ref, bs_ref,
                   a_ref, ob_ref, sg_ref):
    tt = x_ref.shape[1]
    d = x_ref.shape[2]
    for r0 in range(0, tt, INPROJ_SUB):
        rsub = slice(r0, r0 + INPROJ_SUB)
        h = (_norm0(x_ref[0, rsub, :]) * (1.0 + scale_ref[0]) + shift_ref[0]).astype(BF16)
        proj = lambda lo, hi: jnp.dot(h, w_ref[:, lo:hi], preferred_element_type=F32)
        a_ref[0, rsub, :] = proj(0, D_A)
        gu = _gelu(proj(D_A, D_A + D_B)).astype(BF16)
        vn = (_norm0(_gelu(proj(D_A + D_B, D_A + 2 * D_B))) * lng_ref[...] + lnb_ref[...]).astype(BF16)
        for c in range(INPROJ_SUB // CHUNK):
            rows = slice(c * CHUNK, (c + 1) * CHUNK)
            orow = slice(r0 + c * CHUNK, r0 + (c + 1) * CHUNK)
            for hd in range(SGU_HEADS):
                cols = slice(hd * CHUNK, (hd + 1) * CHUNK)
                mixed = jnp.dot(ws_ref[hd], vn[rows, cols], preferred_element_type=F32) + bs_ref[hd]
                ob_ref[0, orow, cols] = (gu[rows, cols].astype(F32) * mixed).astype(BF16)
        g0 = D_A + 2 * D_B
        for t in range(2 * d // INPROJ_TILE):
            cols = slice(t * INPROJ_TILE, (t + 1) * INPROJ_TILE)
            sg_ref[0, rsub, cols] = _sigmoid(proj(g0 + t * INPROJ_TILE, g0 + (t + 1) * INPROJ_TILE)).astype(BF16)


def _inproj(x, shift1, scale1, w_in, lng, lnb, ws, bs_full, tt):
    b, s, d = x.shape
    n = w_in.shape[1]
    const = lambda shape: pl.BlockSpec(shape, lambda bi, i: (0,) * len(shape), pipeline_mode=pl.Buffered(1))
    return pl.pallas_call(
        _inproj_kernel,
        grid=(b, s // tt),
        in_specs=[pl.BlockSpec((1, tt, d), lambda bi, i: (bi, i, 0)),
                  pl.BlockSpec((1, 1, d), lambda bi, i: (bi, 0, 0)),
                  pl.BlockSpec((1, 1, d), lambda bi, i: (bi, 0, 0)),
                  const((d, n)),
                  const((1, D_B)),
                  const((1, D_B)),
                  const((SGU_HEADS, CHUNK, CHUNK)),
                  const((SGU_HEADS, CHUNK, CHUNK))],
        out_specs=[pl.BlockSpec((1, tt, D_A), lambda bi, i: (bi, i, 0)),
                   pl.BlockSpec((1, tt, D_B), lambda bi, i: (bi, i, 0)),
                   pl.BlockSpec((1, tt, 2 * d), lambda bi, i: (bi, i, 0))],
        out_shape=[jax.ShapeDtypeStruct((b, s, D_A), F32),
                   jax.ShapeDtypeStruct((b, s, D_B), BF16),
                   jax.ShapeDtypeStruct((b, s, 2 * d), BF16)],
        compiler_params=_params(("arbitrary", "arbitrary")),
        name="inproj",
    )(x, shift1, scale1, w_in, lng, lnb, ws, bs_full)


def _merge_kernel(x_ref, a_ref, ap_ref, an_ref, ob_ref, sg_ref, pw_ref, ps_ref, wua_ref, wub_ref,
                  wo_ref, gate1_ref, g1_ref, b1_ref, shift2_ref, scale2_ref,
                  x1_ref, h2t_ref, ext_scr, *, seq_len):
    i = pl.program_id(1)
    ni = pl.num_programs(1)
    tt = x_ref.shape[1]
    d = x_ref.shape[2]

    ext_scr[0:HALO, :] = jnp.where(i > 0, ap_ref[0], 0.0)
    ext_scr[HALO:HALO + tt, :] = a_ref[0]
    ext_scr[HALO + tt:HALO + tt + HALO, :] = jnp.where(i < ni - 1, an_ref[0], 0.0)

    for r0 in range(0, tt, MERGE_SUB):
        rows = slice(r0, r0 + MERGE_SUB)
        pos = i * tt + r0 + lax.broadcasted_iota(jnp.int32, (MERGE_SUB, A_GROUP), 0)
        ya = jnp.zeros((MERGE_SUB, d), F32)
        for g, hw in enumerate(POOL_HALF):
            cols = slice(g * A_GROUP, (g + 1) * A_GROUP)
            base = HALO + r0
            win = ext_scr[base - hw:base - hw + MERGE_SUB, cols]
            for off in range(-hw + 1, hw):
                win = win + ext_scr[base + off:base + off + MERGE_SUB, cols]
            cnt = (jnp.minimum(pos + hw, seq_len) - jnp.maximum(pos - hw, 0)).astype(F32)
            pooled = win / cnt - ext_scr[base:base + MERGE_SUB, cols]
            mixed = jnp.dot(pooled.astype(BF16), pw_ref[g], preferred_element_type=F32) * ps_ref[:, cols]
            ya = ya + jnp.dot(mixed.astype(BF16), wua_ref[cols, :], preferred_element_type=F32)

        yb = jnp.dot(ob_ref[0, rows, :], wub_ref[...], preferred_element_type=F32)
        merged = sg_ref[0, rows, 0:d].astype(F32) * ya + sg_ref[0, rows, d:2 * d].astype(F32) * yb
        y = jnp.dot(merged.astype(BF16), wo_ref[...], preferred_element_type=F32)
        x1 = _norm0(ALPHA * x_ref[0, rows, :] + gate1_ref[0] * y) * g1_ref[...] + b1_ref[...]
        x1_ref[0, rows, :] = x1
        h2 = _norm0(x1) * (1.0 + scale2_ref[0]) + shift2_ref[0]
        h2t_ref[0, :, rows] = h2.T.astype(BF16)


def _merge(x, a, ob, sg, pool_w, pool_scale, w_up_a, w_up_b, w_out, gate1, g1, b1, shift2, scale2, tt):
    b, s, d = x.shape
    nh = tt // HALO
    last = s // HALO - 1
    vec = lambda: pl.BlockSpec((1, 1, d), lambda bi, i: (bi, 0, 0))
    row = lambda n: pl.BlockSpec((1, n), lambda bi, i: (0, 0))
    return pl.pallas_call(
        functools.partial(_merge_kernel, seq_len=s),
        grid=(b, s // tt),
        in_specs=[pl.BlockSpec((1, tt, d), lambda bi, i: (bi, i, 0)),
                  pl.BlockSpec((1, tt, D_A), lambda bi, i: (bi, i, 0)),
                  pl.BlockSpec((1, HALO, D_A), lambda bi, i: (bi, jnp.maximum(i * nh - 1, 0), 0)),
                  pl.BlockSpec((1, HALO, D_A), lambda bi, i: (bi, jnp.minimum((i + 1) * nh, last), 0)),
                  pl.BlockSpec((1, tt, D_B), lambda bi, i: (bi, i, 0)),
                  pl.BlockSpec((1, tt, 2 * d), lambda bi, i: (bi, i, 0)),
                  pl.BlockSpec((len(POOL_HALF), A_GROUP, A_GROUP), lambda bi, i: (0, 0, 0)),
                  row(D_A),
                  pl.BlockSpec((D_A, d), lambda bi, i: (0, 0), pipeline_mode=pl.Buffered(1)),
                  pl.BlockSpec((D_B, d), lambda bi, i: (0, 0), pipeline_mode=pl.Buffered(1)),
                  pl.BlockSpec((d, d), lambda bi, i: (0, 0), pipeline_mode=pl.Buffered(1)),
                  vec(), row(d), row(d), vec(), vec()],
        out_specs=[pl.BlockSpec((1, tt, d), lambda bi, i: (bi, i, 0)),
                   pl.BlockSpec((1, d, tt), lambda bi, i: (bi, 0, i))],
        out_shape=[jax.ShapeDtypeStruct((b, s, d), F32),
                   jax.ShapeDtypeStruct((b, d, s), BF16)],
        scratch_shapes=[pltpu.VMEM((tt + 2 * HALO, D_A), F32)],
        compiler_params=_params(("arbitrary", "arbitrary")),
        name="merge",
    )(x, a, a, a, ob, sg, pool_w, pool_scale, w_up_a, w_up_b, w_out, gate1, g1, b1, shift2, scale2)


def _bitonic_merge(xs, descending):
    n = len(xs)
    if n == 1:
        return xs
    h = n // 2
    big = [jnp.maximum(xs[i], xs[i + h]) for i in range(h)]
    small = [jnp.minimum(xs[i], xs[i + h]) for i in range(h)]
    first, second = (big, small) if descending else (small, big)
    return _bitonic_merge(first, descending) + _bitonic_merge(second, descending)


def _bitonic_sort(xs, descending=True):
    n = len(xs)
    if n == 1:
        return xs
    h = n // 2
    return _bitonic_merge(_bitonic_sort(xs[:h], True) + _bitonic_sort(xs[h:], False), descending)


def _merge_top(a, b):
    n = len(a)
    return _bitonic_merge([jnp.maximum(a[i], b[n - 1 - i]) for i in range(n)], True)


def _top_sorted(xs, k):
    groups = [_bitonic_sort(xs[g:g + k]) for g in range(0, len(xs), k)]
    while len(groups) > 1:
        groups = [_merge_top(groups[g], groups[g + 1]) for g in range(0, len(groups), 2)]
    return groups[0]


def _route_kernel(h2t_ref, kq_ref, a_ref, thr_ref, b_ref, s2_scr):
    tt = h2t_ref.shape[2]
    nk, nh = N_KEYS, PEER_HEADS
    half_rows = nk * nh
    neg = jnp.full((nh, LANES), NEG_INF, F32)
    per_piece = MXU_WIDTH // LANES
    piece = lambda p: jnp.dot(kq_ref[...], h2t_ref[0, :, p * MXU_WIDTH:(p + 1) * MXU_WIDTH],
                              preferred_element_type=F32)
    pieces = [piece(0)]

    for c in range(tt // LANES):
        lanes = slice(c * LANES, (c + 1) * LANES)
        if c % per_piece == 1 and len(pieces) < tt // MXU_WIDTH:
            pieces.append(piece(len(pieces)))
        s_all = pieces[c // per_piece][:, (c % per_piece) * LANES:(c % per_piece + 1) * LANES]
        s1 = [s_all[n * nh:(n + 1) * nh, :] for n in range(nk)]
        s2 = [s_all[half_rows + n * nh:half_rows + (n + 1) * nh, :] for n in range(nk)]
        v1 = _top_sorted(s1, TOPK)
        v2 = _top_sorted(s2, TOPK)

        row = lambda k1: [v1[k1] + v2[k2] for k2 in range(TOPK // (k1 + 1))]
        l0 = row(0)
        l1 = row(1) + [neg] * 8
        l2 = _bitonic_sort(row(2) + row(3) + row(4) + [neg] * 4)
        l3 = _bitonic_sort(sum([row(k1) for k1 in range(5, TOPK)], []) + [neg] * 2)
        tau = _merge_top(_merge_top(l0, l1), _merge_top(l2, l3))[TOPK - 1]

        t0 = tau - v2[0]
        e1v = [jnp.exp(v - v1[0]) for v in v1]
        e2v = [jnp.exp(v - v2[0]) for v in v2]
        thrv = [jnp.exp(t0 - v) for v in v1]
        z = jnp.zeros_like(tau)
        for (k1, k2) in _CAND:
            z = z + jnp.where(e2v[k2] >= thrv[k1], e1v[k1] * e2v[k2], 0.0)
        inv_z = 1.0 / z

        for n in range(nk):
            a_ref[0, n, :, lanes] = jnp.where(s1[n] >= v1[TOPK - 1], jnp.exp(s1[n] - v1[0]), 0.0) * inv_z
            thr_ref[0, n, :, lanes] = jnp.exp(t0 - s1[n])

        s2_scr[...] = s_all[half_rows:2 * half_rows, :]
        tau2 = v2[TOPK - 1]
        for hd in range(nh):
            s2h = s2_scr[pl.ds(hd, nk, stride=nh), :]
            b_ref[0, hd, :, lanes] = jnp.where(s2h >= tau2[hd:hd + 1, :],
                                               jnp.exp(s2h - v2[0][hd:hd + 1, :]), 0.0)


def _route(h2t, kq, tt):
    b, d, s = h2t.shape
    nk, nh = N_KEYS, PEER_HEADS
    full = lambda: pl.BlockSpec((d, d), lambda bi, i: (0, 0), pipeline_mode=pl.Buffered(1))
    eh = lambda: pl.BlockSpec((1, nk, nh, tt), lambda bi, i: (bi, 0, 0, i))
    he = lambda: pl.BlockSpec((1, nh, nk, tt), lambda bi, i: (bi, 0, 0, i))
    return pl.pallas_call(
        _route_kernel,
        grid=(b, s // tt),
        in_specs=[pl.BlockSpec((1, d, tt), lambda bi, i: (bi, 0, i)), full()],
        out_specs=[eh(), eh(), he()],
        out_shape=[jax.ShapeDtypeStruct((b, nk, nh, s), F32),
                   jax.ShapeDtypeStruct((b, nk, nh, s), F32),
                   jax.ShapeDtypeStruct((b, nh, nk, s), F32)],
        scratch_shapes=[pltpu.VMEM((nk * nh, LANES), F32)],
        compiler_params=_params(("arbitrary", "arbitrary")),
        name="route",
    )(h2t, kq)


def _experts_step(h2t_ref, u_ref, vt_ref, a_ref, thr_ref, b_ref, acc_scr, xw_scr, hu_next, hu_cur):
    te = u_ref.shape[0]
    tt = h2t_ref.shape[2]
    nsub = te // EXPERT_SUB
    for half in range(tt // EXPERT_PIECE):
        cols = slice(half * EXPERT_PIECE, (half + 1) * EXPERT_PIECE)
        for sub in range(nsub):
            rows = slice(sub * EXPERT_SUB, (sub + 1) * EXPERT_SUB)
            hu_next[rows, cols] = jnp.dot(u_ref[rows, :], h2t_ref[0, :, cols], preferred_element_type=F32)
            for l in range(EXPERT_SUB // N_KEYS):
                e1 = sub * (EXPERT_SUB // N_KEYS) + l
                for c in range(EXPERT_PIECE // LANES):
                    lo = half * EXPERT_PIECE + c * LANES
                    lanes = slice(lo, lo + LANES)
                    thr = [jnp.broadcast_to(thr_ref[0, e1, hd:hd + 1, lanes], (PACK_ROWS, LANES))
                           for hd in range(PEER_HEADS)]
                    a = [jnp.broadcast_to(a_ref[0, e1, hd:hd + 1, lanes], (PACK_ROWS, LANES))
                         for hd in range(PEER_HEADS)]
                    for r in range(N_KEYS // PACK_ROWS):
                        k2 = slice(r * PACK_ROWS, (r + 1) * PACK_ROWS)
                        w = None
                        for hd in range(PEER_HEADS):
                            bv = b_ref[0, hd, k2, lanes]
                            term = jnp.where(bv >= thr[hd], bv, 0.0) * a[hd]
                            w = term if w is None else w + term
                        k2s = slice(e1 * N_KEYS + r * PACK_ROWS, e1 * N_KEYS + (r + 1) * PACK_ROWS)
                        xw_scr[k2s, lanes] = (w * _gelu(hu_cur[k2s, lanes])).astype(BF16)
            acc_scr[cols, :] += lax.dot_general(xw_scr[rows, cols], vt_ref[rows, :],
                                                (((0,), (0,)), ((), ())), preferred_element_type=F32)


def _experts_kernel(h2t_ref, u_ref, vt_ref, a_ref, thr_ref, b_ref, x1_ref, gate2_ref,
                    g2_ref, b2_ref, y_ref, acc_scr, hu_even, hu_odd, xw_scr, *, ne):
    s = pl.program_id(1)

    @pl.when(s == 0)
    def _():
        acc_scr[...] = jnp.zeros_like(acc_scr)
        hu_odd[...] = jnp.zeros_like(hu_odd)

    step = functools.partial(_experts_step, h2t_ref, u_ref, vt_ref, a_ref, thr_ref, b_ref,
                             acc_scr, xw_scr)

    @pl.when(s % 2 == 0)
    def _():
        step(hu_even, hu_odd)

    @pl.when(s % 2 == 1)
    def _():
        step(hu_odd, hu_even)

    @pl.when((s >= 1) & ((s - 1) % ne == ne - 1))
    def _():
        y2 = acc_scr[...]
        y_ref[0] = _norm0(ALPHA * x1_ref[0] + gate2_ref[0] * y2) * g2_ref[...] + b2_ref[...]
        acc_scr[...] = jnp.zeros_like(acc_scr)


def _experts(h2t, u_bf, vt_bf, a, thr, bm, x1, gate2, g2, b2, tt, te):
    b, d, s = h2t.shape
    nk, nh = N_KEYS, PEER_HEADS
    nl = te // nk
    ni = s // tt
    ne = N_EXPERTS // te
    nxt_i = lambda st: jnp.minimum(st // ne, ni - 1)
    cur = lambda st: jnp.maximum(st - 1, 0)
    eh = lambda: pl.BlockSpec((1, nl, nh, tt), lambda bi, st: (bi, cur(st) % ne, 0, cur(st) // ne))
    he = lambda: pl.BlockSpec((1, nh, nk, tt), lambda bi, st: (bi, 0, 0, cur(st) // ne))
    row = lambda: pl.BlockSpec((1, d), lambda bi, st: (0, 0))
    return pl.pallas_call(
        functools.partial(_experts_kernel, ne=ne),
        grid=(b, ni * ne + 1),
        in_specs=[pl.BlockSpec((1, d, tt), lambda bi, st: (bi, 0, nxt_i(st))),
                  pl.BlockSpec((te, d), lambda bi, st: (st % ne, 0)),
                  pl.BlockSpec((te, d), lambda bi, st: (cur(st) % ne, 0)),
                  eh(), eh(), he(),
                  pl.BlockSpec((1, tt, d), lambda bi, st: (bi, cur(st) // ne, 0)),
                  pl.BlockSpec((1, 1, d), lambda bi, st: (bi, 0, 0)),
                  row(), row()],
        out_specs=pl.BlockSpec((1, tt, d), lambda bi, st: (bi, cur(st) // ne, 0)),
        out_shape=jax.ShapeDtypeStruct((b, s, d), F32),
        scratch_shapes=[pltpu.VMEM((tt, d), F32),
                        pltpu.VMEM((te, tt), F32),
                        pltpu.VMEM((te, tt), F32),
                        pltpu.VMEM((te, tt), BF16)],
        compiler_params=_params(("arbitrary", "arbitrary")),
        name="experts",
    )(h2t, u_bf, vt_bf, a, thr, bm, x1, gate2, g2, b2)


def _block_diag_keys(keys):
    nh, _, nk, c = keys.shape
    eye = jnp.eye(nh, dtype=keys.dtype)
    full = jnp.einsum("hknc,hg,kj->knhgjc", keys, eye, jnp.eye(2, dtype=keys.dtype))
    return full.reshape(2 * nk * nh, nh * 2 * c)


def _trunk(x, mod, wts, tiles):
    b, s, d = x.shape
    shift1, scale1, gate1, shift2, scale2, gate2 = [mod[:, k][:, None, :] for k in range(6)]
    t_in, t_merge, t_route, t_exp, t_e = tiles
    a, ob, sg = _inproj(x, shift1, scale1, wts["w_in"], wts["sgu_ln_g"], wts["sgu_ln_b"],
                        wts["sgu_w"], wts["sgu_b_full"], min(t_in, s))
    x1, h2t = _merge(x, a, ob, sg, wts["pool_w"], wts["pool_scale"], wts["w_up_a"], wts["w_up_b"],
                     wts["w_out"], gate1, wts["ln1_g"], wts["ln1_b"], shift2, scale2, min(t_merge, s))
    ra, rthr, rb = _route(h2t, wts["kq"], min(t_route, s))
    return _experts(h2t, wts["u_bf"], wts["vt_bf"], ra, rthr, rb, x1, gate2,
                    wts["ln2_g"], wts["ln2_b"], min(t_exp, s), t_e)


def _prepare(w_ada, b_ada, w_in, pool_w, pool_scale, sgu_ln_g, sgu_ln_b, sgu_w, sgu_b, w_up_a, w_up_b,
             w_out, ln1_g, ln1_b, peer_wq, peer_keys, peer_u, peer_v, ln2_g, ln2_b):
    l = 0
    return {
        "w_in": w_in[l].astype(BF16),
        "pool_w": pool_w[l].astype(BF16),
        "pool_scale": pool_scale[l][None, :],
        "sgu_ln_g": sgu_ln_g[l][None, :],
        "sgu_ln_b": sgu_ln_b[l][None, :],
        "sgu_w": sgu_w[l].astype(BF16),
        "sgu_b_full": jnp.broadcast_to(sgu_b[l][:, :, None], (SGU_HEADS, CHUNK, CHUNK)),
        "w_up_a": w_up_a[l].astype(BF16),
        "w_up_b": w_up_b[l].astype(BF16),
        "w_out": w_out[l].astype(BF16),
        "ln1_g": ln1_g[l][None, :],
        "ln1_b": ln1_b[l][None, :],
        "kq": _fold(_block_diag_keys(peer_keys[l]).astype(BF16), _transpose_cast(peer_wq[l])),
        "u_bf": peer_u[l].astype(BF16),
        "vt_bf": peer_v[l].astype(BF16),
        "ln2_g": ln2_g[l][None, :],
        "ln2_b": ln2_b[l][None, :],
    }


_TILES = (512, 512, 512, 512, 1024)


def kernel(x_prompt, x_sample, c_prompt, c_sample, w_ada, b_ada, w_in, pool_w, pool_scale, sgu_ln_g,
           sgu_ln_b, sgu_w, sgu_b, w_up_a, w_up_b, w_out, ln1_g, ln1_b, peer_wq, peer_keys, peer_u,
           peer_v, ln2_g, ln2_b):
    assert w_ada.shape[0] == DEPTH == 1
    bp = x_prompt.shape[0]
    bs = x_sample.shape[0]
    c_all = jnp.concatenate([c_prompt, c_sample], axis=0)
    c8 = jnp.pad(c_all, ((0, 8 - bp - bs), (0, 0)))
    mod = _ada(c8, w_ada[0], b_ada[0][None, :]).reshape(8, 6, D_MODEL)
    wts = _prepare(w_ada, b_ada, w_in, pool_w, pool_scale, sgu_ln_g, sgu_ln_b, sgu_w, sgu_b, w_up_a,
                   w_up_b, w_out, ln1_g, ln1_b, peer_wq, peer_keys, peer_u, peer_v, ln2_g, ln2_b)
    y_prompt = _trunk(x_prompt, mod[:bp], wts, _TILES)
    y_sample = _trunk(x_sample, mod[bp:bp + bs], wts, _TILES)
    return (y_prompt, y_sample)
```
